```python
import math
import jax, jax.numpy as jnp
from jax import lax
import numpy as np

D_MODEL = 4096
BATCH = 4
SEQ = 2048
DEPTH = 2

D_MIX = D_MODEL
HEAD_DIM = 128
ATTN_WIDTH = 3 * D_MIX // 4
N_ATTN_HEADS = ATTN_WIDTH // HEAD_DIM
POOL_WIDTH = D_MIX - ATTN_WIDTH
POOL_WINDOWS = (2, 4, 8, 16)
N_POOL_GROUPS = len(POOL_WINDOWS)
POOL_GROUP = POOL_WIDTH // N_POOL_GROUPS
D_IN = 3 * ATTN_WIDTH + N_ATTN_HEADS + POOL_WIDTH
Q_BLOCK = 128
N_MOD = 6
D_FF = 256 * ((8 * D_MODEL // 3 + 255) // 256)
N_EXPERTS = 8
TOP_K = 2
EXPERT_FF = D_MODEL
N_DENSE = (DEPTH + 1) // 2
N_MOE = DEPTH // 2
RMS_EPS = 1e-6

kernel_name = "hybrid_fox_pool_moe_adaln_block"


def rmsnorm(x, gain):
    xf = x.astype(jnp.float32)
    inv = lax.rsqrt(jnp.mean(xf * xf, axis=-1, keepdims=True) + RMS_EPS)
    return (xf * inv * gain.astype(jnp.float32)).astype(x.dtype)


def forgetting_attention(q, k, v, log_f):
    S = q.shape[1]
    F = jnp.transpose(jnp.cumsum(log_f, axis=1), (0, 2, 1))
    scale = HEAD_DIM ** -0.5
    outs = []
    for i in range(S // Q_BLOCK):
        q0, q1 = i * Q_BLOCK, (i + 1) * Q_BLOCK
        qb, kb, vb = q[:, q0:q1], k[:, :q1], v[:, :q1]
        s = jnp.einsum('bqhd,bkhd->bhqk', qb, kb).astype(jnp.float32) * scale
        s = s + F[:, :, q0:q1, None] - F[:, :, None, :q1]
        causal = jnp.arange(q1)[None, :] <= jnp.arange(q0, q1)[:, None]
        s = jnp.where(causal, s, -jnp.inf)
        p = jax.nn.softmax(s, axis=-1).astype(v.dtype)
        outs.append(jnp.einsum('bhqk,bkhd->bqhd', p, vb))
    return jnp.concatenate(outs, axis=1)


def pool_mixer(u, w_pool, pool_scale):
    B, S, _ = u.shape
    ug = u.reshape(B, S, N_POOL_GROUPS, POOL_GROUP).astype(jnp.float32)
    cs = jnp.concatenate([jnp.zeros((B, 1, N_POOL_GROUPS, POOL_GROUP), jnp.float32),
                          jnp.cumsum(ug, axis=1)], axis=1)
    t1 = jnp.arange(1, S + 1, dtype=jnp.float32)
    pooled = []
    for g, w in enumerate(POOL_WINDOWS):
        hi = cs[:, 1:, g]
        lo = jnp.pad(cs[:, :S + 1 - w, g], ((0, 0), (w - 1, 0), (0, 0)))
        count = jnp.minimum(t1, float(w))[None, :, None]
        pooled.append((hi - lo) / count - ug[:, :, g])
    pooled = jnp.stack(pooled, axis=2).astype(u.dtype)
    y = jnp.einsum('bsgc,gcd->bsgd', pooled, w_pool)
    return y.reshape(B, S, POOL_WIDTH) * pool_scale


def swiglu(h, wg, wu, wd):
    return (jax.nn.silu(h @ wg) * (h @ wu)) @ wd


def moe_swiglu(h, w_router, we_gate, we_up, we_down):
    logits = (h @ w_router).astype(jnp.float32)
    top_val, top_idx = lax.top_k(logits, TOP_K)
    top_w = jax.nn.softmax(top_val, axis=-1)
    gates = jnp.sum(jax.nn.one_hot(top_idx, N_EXPERTS, dtype=jnp.float32)
                    * top_w[..., None], axis=1).astype(h.dtype)
    y = jnp.zeros_like(h)
    for e in range(N_EXPERTS):
        y = y + gates[:, e:e + 1] * swiglu(h, we_gate[e], we_up[e], we_down[e])
    return y


def setup_inputs(seed: int = 0) -> dict:
    key = jax.random.key(seed)
    ks = jax.random.split(key, 20)
    D = D_MODEL
    nrm = lambda k, shape, fan_in: jax.random.normal(k, shape, jnp.float32) * fan_in ** -0.5
    return {
        "x": jax.random.normal(ks[0], (BATCH, SEQ, D), jnp.float32),
        "c": jax.random.normal(ks[1], (BATCH, D), jnp.float32),
        "norm_gains": 1.0 + 0.1 * jax.random.normal(ks[2], (DEPTH, 4, D), jnp.float32),
        "w_mod": nrm(ks[3], (DEPTH, D, N_MOD * D), D),
        "b_mod": 0.02 * jax.random.normal(ks[4], (DEPTH, N_MOD * D), jnp.float32),
        "w_in": nrm(ks[5], (DEPTH, D, D_IN), D),
        "b_forget": jax.random.uniform(ks[6], (DEPTH, N_ATTN_HEADS), jnp.float32, 1.0, 5.0),
        "w_pool": nrm(ks[7], (DEPTH, N_POOL_GROUPS, POOL_GROUP, POOL_GROUP), POOL_GROUP),
        "pool_scale": 1.0 + 0.1 * jax.random.normal(ks[8], (DEPTH, POOL_WIDTH), jnp.float32),
        "w_out": nrm(ks[9], (DEPTH, D_MIX, D), D_MIX),
        "w_ffn_gate": nrm(ks[10], (N_DENSE, D, D_FF), D),
        "w_ffn_up": nrm(ks[11], (N_DENSE, D, D_FF), D),
        "w_ffn_down": nrm(ks[12], (N_DENSE, D_FF, D), D_FF),
        "w_router": nrm(ks[13], (N_MOE, D, N_EXPERTS), D),
        "w_exp_gate": nrm(ks[14], (N_MOE, N_EXPERTS, D, EXPERT_FF), D),
        "w_exp_up": nrm(ks[15], (N_MOE, N_EXPERTS, D, EXPERT_FF), D),
        "w_exp_down": nrm(ks[16], (N_MOE, N_EXPERTS, EXPERT_FF, D), EXPERT_FF),
    }


def reference(x, c, norm_gains, w_mod, b_mod, w_in, b_forget, w_pool, pool_scale, w_out,
              w_ffn_gate, w_ffn_up, w_ffn_down, w_router, w_exp_gate, w_exp_up, w_exp_down):
    B, S, D = x.shape
    c_act = jax.nn.silu(c)
    for l in range(DEPTH):
        g = norm_gains[l]
        mod = (c_act @ w_mod[l] + b_mod[l]).reshape(B, N_MOD, D)[:, :, None, :]
        shift_m, scale_m, gate_m, shift_f, scale_f, gate_f = [mod[:, i] for i in range(N_MOD)]

        h = rmsnorm(x, g[0]) * (1.0 + scale_m) + shift_m
        proj = h @ w_in[l]
        A, H = ATTN_WIDTH, N_ATTN_HEADS
        q = proj[..., :A].reshape(B, S, H, HEAD_DIM)
        k = proj[..., A:2 * A].reshape(B, S, H, HEAD_DIM)
        v = proj[..., 2 * A:3 * A].reshape(B, S, H, HEAD_DIM)
        log_f = jax.nn.log_sigmoid(proj[..., 3 * A:3 * A + H].astype(jnp.float32)
                                   + b_forget[l].astype(jnp.float32))
        u = proj[..., 3 * A + H:]
        attn = forgetting_attention(q, k, v, log_f).reshape(B, S, A)
        pooled = pool_mixer(u, w_pool[l], pool_scale[l])
        mixed = jnp.concatenate([attn, pooled], axis=-1) @ w_out[l]
        x = x + gate_m * rmsnorm(mixed, g[1])

        h = (rmsnorm(x, g[2]) * (1.0 + scale_f) + shift_f).reshape(B * S, D)
        if l % 2 == 0:
            i = l // 2
            y = swiglu(h, w_ffn_gate[i], w_ffn_up[i], w_ffn_down[i])
        else:
            i = l // 2
            y = moe_swiglu(h, w_router[i], w_exp_gate[i], w_exp_up[i], w_exp_down[i])
        x = x + gate_f * rmsnorm(y.reshape(B, S, D), g[3])
    return x
```

```python
import functools

import jax
import jax.numpy as jnp
from jax import lax
from jax.experimental import pallas as pl
from jax.experimental.pallas import tpu as pltpu

F32 = jnp.float32
BF16 = jnp.bfloat16
I32 = jnp.int32

HEAD_DIM = 128
POOL_WINDOWS = (2, 4, 8, 16)
N_MOD = 6
TOP_K = 2
RMS_EPS = 1e-6

LANES = 128
SUBLANES = 8
V7X_VMEM_BYTES = 64 * 2**20
VMEM_CAP_BYTES = V7X_VMEM_BYTES - 6 * 2**20


def _params(semantics, vmem_bytes):
    limit = int(min(max(vmem_bytes * 5 // 4, 16 * 2**20), VMEM_CAP_BYTES))
    return pltpu.CompilerParams(dimension_semantics=semantics, vmem_limit_bytes=limit)


def _rms(x, gain):
    inv = lax.rsqrt(jnp.mean(x * x, axis=-1, keepdims=True) + RMS_EPS)
    return x * inv * gain


def _mod_kernel(c_ref, w_ref, b_ref, o_ref):
    c = c_ref[...]
    c_act = (c * jax.nn.sigmoid(c)).astype(BF16)
    o_ref[...] = jnp.dot(c_act, w_ref[...].astype(BF16),
                         preferred_element_type=F32) + b_ref[...]


def _mod(c, w_mod, b_mod):
    B, D = c.shape
    L, _, NM = w_mod.shape
    rows = -(-B // SUBLANES) * SUBLANES
    c_pad = jnp.pad(c, ((0, rows - B), (0, 0)))
    tn = 512
    out = pl.pallas_call(
        _mod_kernel,
        grid=(L, NM // tn),
        in_specs=[
            pl.BlockSpec((rows, D), lambda l, j: (0, 0)),
            pl.BlockSpec((None, D, tn), lambda l, j: (l, 0, j)),
            pl.BlockSpec((None, 1, tn), lambda l, j: (l, 0, j)),
        ],
        out_specs=pl.BlockSpec((None, rows, tn), lambda l, j: (l, 0, j)),
        out_shape=jax.ShapeDtypeStruct((L, rows, NM), F32),
        compiler_params=_params(("arbitrary", "arbitrary"), 3 * D * tn * 4),
        name="adaln_mod",
    )(c_pad, w_mod, b_mod.reshape(L, 1, NM))
    return out[:, :B].reshape(L, B, N_MOD, 1, D)


def _mod_spec(l, idx, D):
    return pl.BlockSpec((None, None, None, 1, D), lambda b, i: (l, b, idx, 0, 0))


def _gain_spec(l, idx, D):
    return pl.BlockSpec((None, None, 1, D), lambda b, i: (l, idx, 0, 0))


def _norm_mod_kernel(x_ref, g_ref, sc_ref, sh_ref, o_ref):
    h = _rms(x_ref[...], g_ref[...])
    o_ref[...] = (h * (1.0 + sc_ref[...]) + sh_ref[...]).astype(o_ref.dtype)


def _norm_mod(x, gains, mod, l, gain_idx, shift_idx, scale_idx, out_dtype):
    B, S, D = x.shape
    ts = min(S, 256)
    row = pl.BlockSpec((None, ts, D), lambda b, i: (b, i, 0))
    return pl.pallas_call(
        _norm_mod_kernel,
        grid=(B, S // ts),
        in_specs=[row, _gain_spec(l, gain_idx, D), _mod_spec(l, scale_idx, D),
                  _mod_spec(l, shift_idx, D)],
        out_specs=row,
        out_shape=jax.ShapeDtypeStruct((B, S, D), out_dtype),
        compiler_params=_params(("arbitrary", "arbitrary"), 4 * ts * D * 4),
        name="pre_norm",
    )(x, gains, mod, mod)


def _split_bf16(v):
    hi = v.astype(BF16)
    lo = (v - hi.astype(F32)).astype(BF16)
    return hi, lo


def _resid_norm_kernel(*refs, emit_h, emit_logits):
    x_ref, y_ref, ga_ref, gate_ref = refs[:4]
    pos = 4
    if emit_h:
        gb_ref, sc_ref, sh_ref = refs[pos:pos + 3]
        pos += 3
    if emit_logits:
        wr_ref = refs[pos]
        pos += 1
    xo_ref = refs[pos]
    pos += 1
    x_new = x_ref[...] + gate_ref[...] * _rms(y_ref[...], ga_ref[...])
    xo_ref[...] = x_new
    if emit_h:
        h_ref = refs[pos]
        pos += 1
        h = _rms(x_new, gb_ref[...]) * (1.0 + sc_ref[...]) + sh_ref[...]
        h_ref[...] = h.astype(h_ref.dtype)
        if emit_logits:
            lg_ref = refs[pos]
            h_hi, h_lo = _split_bf16(h)
            w_hi, w_lo = _split_bf16(wr_ref[...])
            lg_ref[...] = (jnp.dot(h_hi, w_hi, preferred_element_type=F32)
                           + jnp.dot(h_lo, w_hi, preferred_element_type=F32)
                           + jnp.dot(h_hi, w_lo, preferred_element_type=F32))


def _resid_norm(x, y, gains, mod, la, ga_idx, gate_idx, nxt=None, h_dtype=BF16,
                w_router=None):
    B, S, D = x.shape
    ts = min(S, 256)
    row = pl.BlockSpec((None, ts, D), lambda b, i: (b, i, 0))
    in_specs = [row, row, _gain_spec(la, ga_idx, D), _mod_spec(la, gate_idx, D)]
    args = [x, y, gains, mod]
    out_specs = [row]
    out_shape = [jax.ShapeDtypeStruct((B, S, D), F32)]
    emit_h = nxt is not None
    emit_logits = w_router is not None
    if emit_h:
        lb, gb_idx, shift_idx, scale_idx = nxt
        in_specs += [_gain_spec(lb, gb_idx, D), _mod_spec(lb, scale_idx, D),
                     _mod_spec(lb, shift_idx, D)]
        args += [gains, mod, mod]
        out_specs.append(row)
        out_shape.append(jax.ShapeDtypeStruct((B, S, D), h_dtype))
    if emit_logits:
        in_specs.append(pl.BlockSpec((D, LANES), lambda b, i: (0, 0)))
        args.append(w_router)
        out_specs.append(pl.BlockSpec((None, ts, LANES), lambda b, i: (b, i, 0)))
        out_shape.append(jax.ShapeDtypeStruct((B, S, LANES), F32))
    outs = pl.pallas_call(
        functools.partial(_resid_norm_kernel, emit_h=emit_h, emit_logits=emit_logits),
        grid=(B, S // ts),
        in_specs=in_specs,
        out_specs=out_specs,
        out_shape=out_shape,
        compiler_params=_params(("arbitrary", "arbitrary"), 10 * ts * D * 4),
        name="resid_norm",
    )(*args)
    return outs


def _mm_kernel(a_ref, w_ref, o_ref):
    o_ref[...] = jnp.dot(a_ref[...], w_ref[...].astype(BF16),
                         preferred_element_type=F32).astype(o_ref.dtype)


def _mm_colscale_kernel(a_ref, w_ref, cs_ref, o_ref):
    acc = jnp.dot(a_ref[...], w_ref[...].astype(BF16), preferred_element_type=F32)
    o_ref[...] = (acc * cs_ref[...]).astype(o_ref.dtype)


def _gateup_kernel(a_ref, wg_ref, wu_ref, o_ref):
    a = a_ref[...]
    g = jnp.dot(a, wg_ref[...].astype(BF16), preferred_element_type=F32)
    u = jnp.dot(a, wu_ref[...].astype(BF16), preferred_element_type=F32)
    o_ref[...] = (g * jax.nn.sigmoid(g) * u).astype(o_ref.dtype)


def _w_spec(w, tn, lead, col_block_offset=0):
    K = w.shape[-2]
    nlead = len(lead)
    return pl.BlockSpec((None,) * nlead + (K, tn),
                        lambda i, j: tuple(lead) + (0, j + col_block_offset))


def _dense_vmem(tm, K, tn, n_w, w_bytes, out_bytes, a_buffers):
    a = a_buffers * tm * K * 2
    w = n_w * (2 * K * tn * w_bytes + K * tn * 2)
    o = 2 * tm * tn * out_bytes + n_w * tm * tn * 4
    return a + w + o


def _matmul(a, w, lead, n_cols, tm, tn, out_dtype, colscale=None, name="matmul"):
    M, K = a.shape
    in_specs = [pl.BlockSpec((tm, K), lambda i, j: (i, 0)), _w_spec(w, tn, lead)]
    args = [a, w]
    body = _mm_kernel
    if colscale is not None:
        in_specs.append(pl.BlockSpec((1, tn), lambda i, j: (0, j)))
        args.append(colscale)
        body = _mm_colscale_kernel
    return pl.pallas_call(
        body,
        grid=(M // tm, n_cols // tn),
        in_specs=in_specs,
        out_specs=pl.BlockSpec((tm, tn), lambda i, j: (i, j)),
        out_shape=jax.ShapeDtypeStruct((M, n_cols), out_dtype),
        compiler_params=_params(
            ("arbitrary", "arbitrary"),
            _dense_vmem(tm, K, tn, 1, w.dtype.itemsize, jnp.dtype(out_dtype).itemsize, 2)),
        name=name,
    )(*args)


def _gateup(a, wg, wu, lead, tm, tn, name="ffn_gate_up"):
    M, K = a.shape
    n_cols = wg.shape[-1]
    return pl.pallas_call(
        _gateup_kernel,
        grid=(M // tm, n_cols // tn),
        in_specs=[pl.BlockSpec((tm, K), lambda i, j: (i, 0)),
                  _w_spec(wg, tn, lead), _w_spec(wu, tn, lead)],
        out_specs=pl.BlockSpec((tm, tn), lambda i, j: (i, j)),
        out_shape=jax.ShapeDtypeStruct((M, n_cols), BF16),
        compiler_params=_params(("arbitrary", "arbitrary"),
                                _dense_vmem(tm, K, tn, 2, wg.dtype.itemsize, 2, 2)),
        name=name,
    )(a, wg, wu)


def _down(a, w, lead, tm, tn, name="ffn_down"):
    M, K = a.shape
    n_cols = w.shape[-1]
    return pl.pallas_call(
        _mm_kernel,
        grid=(M // tm, n_cols // tn),
        in_specs=[pl.BlockSpec((tm, K), lambda i, j: (i, 0), pipeline_mode=pl.Buffered(1)),
                  _w_spec(w, tn, lead)],
        out_specs=pl.BlockSpec((tm, tn), lambda i, j: (i, j)),
        out_shape=jax.ShapeDtypeStruct((M, n_cols), F32),
        compiler_params=_params(("arbitrary", "arbitrary"),
                                _dense_vmem(tm, K, tn, 1, w.dtype.itemsize, 4, 1)),
        name=name,
    )(a, w)


def _fox_prep_kernel(f_ref, b_ref, fsh_ref, fhs_ref):
    z = f_ref[...] + b_ref[...]
    acc = jnp.minimum(z, 0.0) - jnp.log1p(jnp.exp(-jnp.abs(z)))
    S = acc.shape[0]
    row = lax.broadcasted_iota(I32, acc.shape, 0)
    d = 1
    while d < S:
        acc = acc + jnp.where(row >= d, pltpu.roll(acc, d, 0), 0.0)
        d *= 2
    fsh_ref[...] = acc
    fhs_ref[...] = acc.T


def _fox_prep(fu3, b_pad, col_block):
    B, S, _ = fu3.shape
    return pl.pallas_call(
        _fox_prep_kernel,
        grid=(B,),
        in_specs=[pl.BlockSpec((None, S, LANES), lambda b: (b, 0, col_block)),
                  pl.BlockSpec((1, LANES), lambda b: (0, 0))],
        out_specs=[pl.BlockSpec((None, S, LANES), lambda b: (b, 0, 0)),
                   pl.BlockSpec((None, LANES, S), lambda b: (b, 0, 0))],
        out_shape=[jax.ShapeDtypeStruct((B, S, LANES), F32),
                   jax.ShapeDtypeStruct((B, LANES, S), F32)],
        compiler_params=_params(("arbitrary",), 12 * S * LANES * 4),
        name="fox_prep",
    )(fu3, b_pad)


def _attn_kernel(q_ref, k_ref, v_ref, fq_ref, fk_ref, o_ref, fq_s, m_s, l_s, acc_s, *, blk):
    h = pl.program_id(1)
    i = pl.program_id(2)
    j = pl.program_id(3)

    @pl.when(j == 0)
    def _init():
        lane = lax.broadcasted_iota(I32, fq_ref.shape, 1)
        fq_s[...] = jnp.sum(jnp.where(lane == h, fq_ref[...], 0.0), axis=-1, keepdims=True)
        m_s[...] = jnp.full(m_s.shape, -jnp.inf, F32)
        l_s[...] = jnp.zeros(l_s.shape, F32)
        acc_s[...] = jnp.zeros(acc_s.shape, F32)

    def _step(diagonal):
        s = lax.dot_general(q_ref[...], k_ref[...], (((1,), (1,)), ((), ())),
                            preferred_element_type=F32)
        s = s + (fq_s[...] - fk_ref[...])
        if diagonal:
            r = lax.broadcasted_iota(I32, s.shape, 0)
            c = lax.broadcasted_iota(I32, s.shape, 1)
            s = jnp.where(c <= r, s, -jnp.inf)
        m_prev = m_s[...]
        m_new = jnp.maximum(m_prev, jnp.max(s, axis=-1, keepdims=True))
        alpha = jnp.exp(m_prev - m_new)
        p = jnp.exp(s - m_new)
        l_s[...] = alpha * l_s[...] + jnp.sum(p, axis=-1, keepdims=True)
        acc_s[...] = alpha * acc_s[...] + jnp.dot(p.astype(BF16), v_ref[...],
                                                  preferred_element_type=F32)
        m_s[...] = m_new

    @pl.when(j < i)
    def _off_diagonal():
        _step(False)

    @pl.when(j == i)
    def _diagonal():
        _step(True)
        o_ref[...] = (acc_s[...] / l_s[...]).astype(o_ref.dtype)


def _attention(qkv3, f_sh, f_hs4, n_heads):
    B, S, _ = qkv3.shape
    H = n_heads
    blk = min(S, 512)
    nb = S // blk
    kv_row = lambda i, j: jnp.minimum(j, i)
    scratch_bytes = blk * blk * 4 * 6
    return pl.pallas_call(
        functools.partial(_attn_kernel, blk=blk),
        grid=(B, H, nb, nb),
        in_specs=[
            pl.BlockSpec((None, blk, HEAD_DIM), lambda b, h, i, j: (b, i, h)),
            pl.BlockSpec((None, blk, HEAD_DIM), lambda b, h, i, j: (b, kv_row(i, j), H + h)),
            pl.BlockSpec((None, blk, HEAD_DIM), lambda b, h, i, j: (b, kv_row(i, j), 2 * H + h)),
            pl.BlockSpec((None, blk, LANES), lambda b, h, i, j: (b, i, 0)),
            pl.BlockSpec((None, None, 1, blk), lambda b, h, i, j: (b, h, 0, kv_row(i, j))),
        ],
        out_specs=pl.BlockSpec((None, blk, HEAD_DIM), lambda b, h, i, j: (b, i, h)),
        out_shape=jax.ShapeDtypeStruct((B, S, H * HEAD_DIM), BF16),
        scratch_shapes=[pltpu.VMEM((blk, 1), F32), pltpu.VMEM((blk, 1), F32),
                        pltpu.VMEM((blk, 1), F32), pltpu.VMEM((blk, HEAD_DIM), F32)],
        compiler_params=_params(("arbitrary",) * 4, scratch_bytes),
        name="fox_attention",
    )(qkv3, qkv3, qkv3, f_sh, f_hs4)


def _pool_kernel(u_ref, w_ref, sc_ref, o_ref):
    g = pl.program_id(1)
    u = u_ref[...]
    row = lax.broadcasted_iota(I32, u.shape, 0)

    def shifted(v, d):
        return jnp.where(row >= d, pltpu.roll(v, d, 0), 0.0)

    s2 = u + shifted(u, 1)
    s4 = s2 + shifted(s2, 2)
    s8 = s4 + shifted(s4, 4)
    s16 = s8 + shifted(s8, 8)
    win = jnp.where(g == 0, s2, jnp.where(g == 1, s4, jnp.where(g == 2, s8, s16)))
    width = lax.shift_left(jnp.int32(POOL_WINDOWS[0]), g)
    count = jnp.minimum(row + 1, width).astype(F32)
    pooled = win / count - u
    y = jnp.dot(pooled.astype(BF16), w_ref[...].astype(BF16), preferred_element_type=F32)
    o_ref[...] = (y * sc_ref[...]).astype(o_ref.dtype)


def _pool(fu3, w_pool, pool_scale3, l):
    B, S, _ = fu3.shape
    G, C = w_pool.shape[1], w_pool.shape[2]
    assert G == len(POOL_WINDOWS) and all(
        w == POOL_WINDOWS[0] << k for k, w in enumerate(POOL_WINDOWS))
    return pl.pallas_call(
        _pool_kernel,
        grid=(B, G),
        in_specs=[pl.BlockSpec((None, S, C), lambda b, g: (b, 0, g)),
                  pl.BlockSpec((None, None, C, C), lambda b, g: (l, g, 0, 0)),
                  pl.BlockSpec((None, 1, C), lambda b, g: (l, 0, g))],
        out_specs=pl.BlockSpec((None, S, C), lambda b, g: (b, 0, g)),
        out_shape=jax.ShapeDtypeStruct((B, S, G * C), BF16),
        compiler_params=_params(("arbitrary", "arbitrary"), 14 * S * C * 4),
        name="pool_mixer",
    )(fu3, w_pool, pool_scale3)


def _route_kernel(lg_ref, dest_ref, wts_ref, meta_ref, m_all, m_top, tri, *,
                  n_experts, tile, chunk):
    N = lg_ref.shape[0]
    n_chunks = N // chunk
    lane = lax.broadcasted_iota(I32, (chunk, LANES), 1).astype(F32)
    lane1 = lax.broadcasted_iota(I32, (1, LANES), 1).astype(F32)
    r = lax.broadcasted_iota(I32, (chunk, chunk), 0)
    c = lax.broadcasted_iota(I32, (chunk, chunk), 1)
    tri[...] = jnp.where(c < r, 1.0, 0.0).astype(BF16)

    def select(ci, counts):
        rows = pl.ds(pl.multiple_of(ci * chunk, chunk), chunk)
        lg = jnp.where(lane < n_experts, lg_ref[rows, :], -jnp.inf)
        v1 = jnp.max(lg, axis=-1, keepdims=True)
        i1 = jnp.min(jnp.where(lg == v1, lane, float(LANES)), axis=-1, keepdims=True)
        first = lane == i1
        lg2 = jnp.where(first, -jnp.inf, lg)
        v2 = jnp.max(lg2, axis=-1, keepdims=True)
        i2 = jnp.min(jnp.where(lg2 == v2, lane, float(LANES)), axis=-1, keepdims=True)
        second = lane == i2
        e2 = jnp.exp(v2 - v1)
        w1 = 1.0 / (1.0 + e2)
        w2 = e2 / (1.0 + e2)
        wts_ref[rows, :] = jnp.where(lane == 0.0, w1, jnp.where(lane == 1.0, w2, 0.0))
        both = jnp.where(first | second, 1.0, 0.0)
        m_all[rows, :] = both.astype(BF16)
        m_top[rows, :] = jnp.where(first, 1.0, 0.0).astype(BF16)
        return counts + jnp.sum(both, axis=0, keepdims=True)

    counts = lax.fori_loop(0, n_chunks, select, jnp.zeros((1, LANES), F32))
    padded = jnp.ceil(counts / tile) * tile
    start = jnp.zeros((1, LANES), F32)
    tile_row = lane1 * tile
    tile_expert = jnp.zeros((1, LANES), F32)
    last_used = jnp.zeros((1, 1), F32)
    running = jnp.zeros((1, 1), F32)
    for e in range(n_experts):
        rows_e = jnp.sum(jnp.where(lane1 == e, padded, 0.0), axis=-1, keepdims=True)
        start = start + jnp.where(lane1 > e, rows_e, 0.0)
        running = running + rows_e
        tile_expert = tile_expert + jnp.where(tile_row >= running, 1.0, 0.0)
        last_used = jnp.where(rows_e > 0.0, float(e), last_used)
    tile_expert = jnp.minimum(tile_expert, last_used)
    n_tiles = running / tile
    sub = lax.broadcasted_iota(I32, (SUBLANES, LANES), 0)
    meta_ref[...] = jnp.where(sub == 0, tile_expert, jnp.where(sub == 1, n_tiles, 0.0)).astype(I32)

    def place(ci, carry):
        rows = pl.ds(pl.multiple_of(ci * chunk, chunk), chunk)
        both = m_all[rows, :]
        top = m_top[rows, :].astype(F32)
        rank = jnp.dot(tri[...], both, preferred_element_type=F32)
        pos = start + carry + rank
        both_f = both.astype(F32)
        d1 = jnp.sum(top * pos, axis=-1, keepdims=True)
        d2 = jnp.sum((both_f - top) * pos, axis=-1, keepdims=True)
        dest_ref[rows, :] = jnp.where(lane == 0.0, d1, jnp.where(lane == 1.0, d2, 0.0)).astype(I32)
        return carry + jnp.sum(both_f, axis=0, keepdims=True)

    lax.fori_loop(0, n_chunks, place, jnp.zeros((1, LANES), F32))


def _route(logits, n_experts, tile):
    N = logits.shape[0]
    chunk = min(N, 512)
    return pl.pallas_call(
        functools.partial(_route_kernel, n_experts=n_experts, tile=tile, chunk=chunk),
        out_shape=[jax.ShapeDtypeStruct((N, LANES), I32),
                   jax.ShapeDtypeStruct((N, LANES), F32),
                   jax.ShapeDtypeStruct((SUBLANES, LANES), I32)],
        scratch_shapes=[pltpu.VMEM((N, LANES), BF16), pltpu.VMEM((N, LANES), BF16),
                        pltpu.VMEM((chunk, chunk), BF16)],
        compiler_params=pltpu.CompilerParams(
            vmem_limit_bytes=int(min(max(N * LANES * 4 * 8, 16 * 2**20), VMEM_CAP_BYTES))),
        name="moe_route",
    )(logits)


def _gather_kernel(dest_ref, nt_ref, h_hbm, o_ref, src, stage, sem, *, tile, n_rows, n_assign):
    t = pl.program_id(0)

    @pl.when(t == 0)
    def _invert_dest():
        def clear(rr, carry):
            src[rr] = 0
            return carry
        lax.fori_loop(0, n_rows, clear, 0)

        def scatter(a, carry):
            src[dest_ref[a]] = a // TOP_K
            return carry
        lax.fori_loop(0, n_assign, scatter, 0)

    @pl.when(t < nt_ref[0])
    def _gather():
        base = t * tile

        def issue(rr, carry):
            tok = src[base + rr]
            pltpu.make_async_copy(h_hbm.at[pl.ds(tok, 1), :], stage.at[pl.ds(rr, 1), :],
                                  sem).start()
            return carry
        lax.fori_loop(0, tile, issue, 0)
        pltpu.make_async_copy(h_hbm.at[pl.ds(0, tile), :], stage, sem).wait()
        o_ref[...] = stage[...].astype(o_ref.dtype)

    @pl.when(t >= nt_ref[0])
    def _unused_tile():
        o_ref[...] = jnp.zeros(o_ref.shape, o_ref.dtype)


def _gather(dest_flat, n_tiles, h2, tile, max_tiles):
    N, D = h2.shape
    n_rows = max_tiles * tile
    grid_spec = pltpu.PrefetchScalarGridSpec(
        num_scalar_prefetch=2,
        grid=(max_tiles,),
        in_specs=[pl.BlockSpec(memory_space=pl.ANY)],
        out_specs=pl.BlockSpec((tile, D), lambda t, dest, nt: (t, 0)),
        scratch_shapes=[pltpu.SMEM((n_rows,), I32), pltpu.VMEM((tile, D), F32),
                        pltpu.SemaphoreType.DMA(())],
    )
    return pl.pallas_call(
        functools.partial(_gather_kernel, tile=tile, n_rows=n_rows, n_assign=dest_flat.shape[0]),
        grid_spec=grid_spec,
        out_shape=jax.ShapeDtypeStruct((n_rows, D), BF16),
        compiler_params=_params(("arbitrary",), 3 * tile * D * 4),
        name="moe_gather",
    )(dest_flat, n_tiles, h2)


def _grouped_gateup_kernel(te_ref, nt_ref, a_ref, wg_ref, wu_ref, o_ref):
    t = pl.program_id(1)

    @pl.when(t < nt_ref[0])
    def _compute():
        _gateup_kernel(a_ref, wg_ref, wu_ref, o_ref)

    @pl.when(t >= nt_ref[0])
    def _unused_tile():
        o_ref[...] = jnp.zeros(o_ref.shape, o_ref.dtype)


def _grouped_down_kernel(te_ref, nt_ref, a_ref, w_ref, o_ref):
    t = pl.program_id(1)

    @pl.when(t < nt_ref[0])
    def _compute():
        _mm_kernel(a_ref, w_ref, o_ref)

    @pl.when(t >= nt_ref[0])
    def _unused_tile():
        o_ref[...] = jnp.zeros(o_ref.shape, o_ref.dtype)


def _grouped(body, a, ws, l, tile_expert, n_tiles, tile, tn, out_dtype, name):
    R, K = a.shape
    n_cols = ws[0].shape[-1]
    max_tiles = R // tile

    def used(t, nt):
        return jnp.minimum(t, nt[0] - 1)

    a_spec = pl.BlockSpec((tile, K), lambda j, t, te, nt: (used(t, nt), 0))
    w_spec = pl.BlockSpec((None, None, K, tn),
                          lambda j, t, te, nt: (l, te[used(t, nt)], 0, j))
    grid_spec = pltpu.PrefetchScalarGridSpec(
        num_scalar_prefetch=2,
        grid=(n_cols // tn, max_tiles),
        in_specs=[a_spec] + [w_spec] * len(ws),
        out_specs=pl.BlockSpec((tile, tn), lambda j, t, te, nt: (t, j)),
    )
    return pl.pallas_call(
        body,
        grid_spec=grid_spec,
        out_shape=jax.ShapeDtypeStruct((R, n_cols), out_dtype),
        compiler_params=_params(
            ("arbitrary", "arbitrary"),
            _dense_vmem(tile, K, tn, len(ws), ws[0].dtype.itemsize,
                        jnp.dtype(out_dtype).itemsize, 2)),
        name=name,
    )(tile_expert, n_tiles, a, *ws)


def _combine_kernel(dest_ref, ys_hbm, wts_ref, x_ref, g_ref, gate_ref, o_ref,
                    stage0, stage1, sems, *, tile, tiles_per_seq):
    n0 = (pl.program_id(0) * tiles_per_seq + pl.program_id(1)) * tile

    def issue(rr, carry):
        a = (n0 + rr) * TOP_K
        pltpu.make_async_copy(ys_hbm.at[pl.ds(dest_ref[a], 1), :],
                              stage0.at[pl.ds(rr, 1), :], sems.at[0]).start()
        pltpu.make_async_copy(ys_hbm.at[pl.ds(dest_ref[a + 1], 1), :],
                              stage1.at[pl.ds(rr, 1), :], sems.at[1]).start()
        return carry
    lax.fori_loop(0, tile, issue, 0)
    pltpu.make_async_copy(ys_hbm.at[pl.ds(0, tile), :], stage0, sems.at[0]).wait()
    pltpu.make_async_copy(ys_hbm.at[pl.ds(0, tile), :], stage1, sems.at[1]).wait()
    w = wts_ref[...]
    y = w[:, 0:1] * stage0[...] + w[:, 1:2] * stage1[...]
    o_ref[...] = x_ref[...] + gate_ref[...] * _rms(y, g_ref[...])


def _combine(dest_flat, ys, wts3, x, gains, mod, l, g_idx, gate_idx):
    B, S, D = x.shape
    tile = min(S, 256)
    row = pl.BlockSpec((None, tile, D), lambda b, i, dest: (b, i, 0))
    grid_spec = pltpu.PrefetchScalarGridSpec(
        num_scalar_prefetch=1,
        grid=(B, S // tile),
        in_specs=[pl.BlockSpec(memory_space=pl.ANY),
                  pl.BlockSpec((None, tile, LANES), lambda b, i, dest: (b, i, 0)),
                  row,
                  pl.BlockSpec((None, None, 1, D), lambda b, i, dest: (l, g_idx, 0, 0)),
                  pl.BlockSpec((None, None, None, 1, D),
                               lambda b, i, dest: (l, b, gate_idx, 0, 0))],
        out_specs=row,
        scratch_shapes=[pltpu.VMEM((tile, D), F32), pltpu.VMEM((tile, D), F32),
                        pltpu.SemaphoreType.DMA((2,))],
    )
    return pl.pallas_call(
        functools.partial(_combine_kernel, tile=tile, tiles_per_seq=S // tile),
        grid_spec=grid_spec,
        out_shape=jax.ShapeDtypeStruct((B, S, D), F32),
        compiler_params=_params(("arbitrary", "arbitrary"), 8 * tile * D * 4),
        name="moe_combine",
    )(dest_flat, ys, wts3, x, gains, mod)


def kernel(x, c, norm_gains, w_mod, b_mod, w_in, b_forget, w_pool, pool_scale, w_out,
           w_ffn_gate, w_ffn_up, w_ffn_down, w_router, w_exp_gate, w_exp_up, w_exp_down):
    B, S, D = x.shape
    L = w_mod.shape[0]
    N = B * S
    H = b_forget.shape[1]
    A = H * HEAD_DIM
    P = pool_scale.shape[1]
    assert A + P == D and w_in.shape[2] == 3 * A + H + P and H <= LANES
    assert P % LANES == 0 and A % LANES == 0
    E = w_router.shape[2]
    tm = min(N, 1024)

    mod = _mod(c, w_mod, b_mod)
    gains = norm_gains.reshape(L, 4, 1, D)
    h = _norm_mod(x, gains, mod, 0, 0, 0, 1, BF16)

    qk_scale = jnp.concatenate([jnp.full((1, A), HEAD_DIM ** -0.5, F32),
                                jnp.ones((1, 2 * A), F32)], axis=1)
    pool_scale3 = pool_scale.reshape(L, 1, P)

    for l in range(L):
        h2 = h.reshape(N, D)
        qkv = _matmul(h2, w_in, (l,), 3 * A, tm, min(3 * A, 512), BF16,
                      colscale=qk_scale, name="in_proj_qkv")
        w_fu = jnp.concatenate([w_in[l, :, 3 * A + H:], w_in[l, :, 3 * A:3 * A + H],
                                jnp.zeros((D, LANES - H), F32)], axis=1)
        fu = _matmul(h2, w_fu, (), P + LANES, tm, LANES * 3 if (P + LANES) % (LANES * 3) == 0 else LANES,
                     F32, name="in_proj_gate_pool")
        fu3 = fu.reshape(B, S, P + LANES)
        b_pad = jnp.pad(b_forget[l], (0, LANES - H)).reshape(1, LANES)
        f_sh, f_hs = _fox_prep(fu3, b_pad, P // LANES)
        attn = _attention(qkv.reshape(B, S, 3 * A), f_sh, f_hs.reshape(B, LANES, 1, S), H)
        pooled = _pool(fu3, w_pool, pool_scale3, l)
        mixed_in = jnp.concatenate([attn, pooled], axis=-1).reshape(N, D)
        mixed = _matmul(mixed_in, w_out, (l,), D, tm, min(D, 512), F32, name="out_proj")

        i = l // 2
        if l % 2 == 0:
            x, h = _resid_norm(x, mixed.reshape(B, S, D), gains, mod, l, 1, 2,
                               nxt=(l, 2, 3, 4))
            FF = w_ffn_gate.shape[2]
            tn_ff = 256 if FF % 256 == 0 else LANES
            act = _gateup(h.reshape(N, D), w_ffn_gate, w_ffn_up, (i,), tm, tn_ff)
            y = _down(act, w_ffn_down, (i,), tm, min(D, 256))
            y3 = y.reshape(B, S, D)
            if l + 1 < L:
                x, h = _resid_norm(x, y3, gains, mod, l, 3, 5, nxt=(l + 1, 0, 0, 1))
            else:
                (x,) = _resid_norm(x, y3, gains, mod, l, 3, 5)
        else:
            wr_pad = jnp.pad(w_router[i], ((0, 0), (0, LANES - E)))
            x, hf, logits = _resid_norm(x, mixed.reshape(B, S, D), gains, mod, l, 1, 2,
                                        nxt=(l, 2, 3, 4), h_dtype=F32, w_router=wr_pad)
            tile = min(N, 512)
            max_tiles = (N * TOP_K) // tile + E
            assert max_tiles <= LANES
            dest, wts, meta = _route(logits.reshape(N, LANES), E, tile)
            dest_flat = dest[:, :TOP_K].reshape(N * TOP_K)
            tile_expert = meta[0, :max_tiles]
            n_tiles = meta[1, :1]
            xs = _gather(dest_flat, n_tiles, hf.reshape(N, D), tile, max_tiles)
            tn_e = min(w_exp_gate.shape[3], 512)
            act = _grouped(_grouped_gateup_kernel, xs, (w_exp_gate, w_exp_up), i,
                           tile_expert, n_tiles, tile, tn_e, BF16, "moe_gate_up")
            ys = _grouped(_grouped_down_kernel, act, (w_exp_down,), i,
                          tile_expert, n_tiles, tile, min(D, 512), F32, "moe_down")
            x = _combine(dest_flat, ys, wts.reshape(B, S, LANES), x, gains, mod, l, 3, 5)
            if l + 1 < L:
                h = _norm_mod(x, gains, mod, l + 1, 0, 0, 1, BF16)
    return x
```

```python
import functools

import jax
import jax.numpy as jnp
from jax import lax
from jax.experimental import pallas as pl
from jax.experimental.pallas import tpu as pltpu

F32 = jnp.float32
BF16 = jnp.bfloat16
I32 = jnp.int32

HEAD_DIM = 128
POOL_WINDOWS = (2, 4, 8, 16)
N_MOD = 6
TOP_K = 2
RMS_EPS = 1e-6

LANES = 128
SUBLANES = 8
V7X_VMEM_BYTES = 64 * 2**20
VMEM_CAP_BYTES = V7X_VMEM_BYTES - 6 * 2**20


def _params(semantics, vmem_bytes):
    limit = int(min(max(vmem_bytes * 5 // 4, 16 * 2**20), VMEM_CAP_BYTES))
    return pltpu.CompilerParams(dimension_semantics=semantics, vmem_limit_bytes=limit)


def _rms(x, gain):
    inv = lax.rsqrt(jnp.mean(x * x, axis=-1, keepdims=True) + RMS_EPS)
    return x * inv * gain


def _mod_kernel(c_ref, w_ref, b_ref, o_ref):
    c = c_ref[...]
    c_act = (c * jax.nn.sigmoid(c)).astype(BF16)
    o_ref[...] = jnp.dot(c_act, w_ref[...].astype(BF16),
                         preferred_element_type=F32) + b_ref[...]


def _mod(c, w_mod, b_mod):
    B, D = c.shape
    L, _, NM = w_mod.shape
    rows = -(-B // SUBLANES) * SUBLANES
    c_pad = jnp.pad(c, ((0, rows - B), (0, 0)))
    tn = 512
    out = pl.pallas_call(
        _mod_kernel,
        grid=(L, NM // tn),
        in_specs=[
            pl.BlockSpec((rows, D), lambda l, j: (0, 0)),
            pl.BlockSpec((None, D, tn), lambda l, j: (l, 0, j)),
            pl.BlockSpec((None, 1, tn), lambda l, j: (l, 0, j)),
        ],
        out_specs=pl.BlockSpec((None, rows, tn), lambda l, j: (l, 0, j)),
        out_shape=jax.ShapeDtypeStruct((L, rows, NM), F32),
        compiler_params=_params(("arbitrary", "arbitrary"), 3 * D * tn * 4),
        name="adaln_mod",
    )(c_pad, w_mod, b_mod.reshape(L, 1, NM))
    return out[:, :B].reshape(L, B, N_MOD, 1, D)


def _mod_spec(l, idx, D):
    return pl.BlockSpec((None, None, None, 1, D), lambda b, i: (l, b, idx, 0, 0))


def _gain_spec(l, idx, D):
    return pl.BlockSpec((None, None, 1, D), lambda b, i: (l, idx, 0, 0))


def _norm_mod_kernel(x_ref, g_ref, sc_ref, sh_ref, o_ref):
    h = _rms(x_ref[...], g_ref[...])
    o_ref[...] = (h * (1.0 + sc_ref[...]) + sh_ref[...]).astype(o_ref.dtype)


def _norm_mod(x, gains, mod, l, gain_idx, shift_idx, scale_idx, out_dtype):
    B, S, D = x.shape
    ts = min(S, 256)
    row = pl.BlockSpec((None, ts, D), lambda b, i: (b, i, 0))
    return pl.pallas_call(
        _norm_mod_kernel,
        grid=(B, S // ts),
        in_specs=[row, _gain_spec(l, gain_idx, D), _mod_spec(l, scale_idx, D),
                  _mod_spec(l, shift_idx, D)],
        out_specs=row,
        out_shape=jax.ShapeDtypeStruct((B, S, D), out_dtype),
        compiler_params=_params(("arbitrary", "arbitrary"), 4 * ts * D * 4),
        name="pre_norm",
    )(x, gains, mod, mod)


def _split_bf16(v):
    hi = v.astype(BF16)
    lo = (v - hi.astype(F32)).astype(BF16)
    return hi, lo


def _resid_norm_kernel(*refs, emit_h, emit_logits):
    x_ref, y_ref, ga_ref, gate_ref = refs[:4]
    pos = 4
    if emit_h:
        gb_ref, sc_ref, sh_ref = refs[pos:pos + 3]
        pos += 3
    if emit_logits:
        wr_ref = refs[pos]
        pos += 1
    xo_ref = refs[pos]
    pos += 1
    x_new = x_ref[...] + gate_ref[...] * _rms(y_ref[...], ga_ref[...])
    xo_ref[...] = x_new
    if emit_h:
        h_ref = refs[pos]
        pos += 1
        h = _rms(x_new, gb_ref[...]) * (1.0 + sc_ref[...]) + sh_ref[...]
        h_ref[...] = h.astype(h_ref.dtype)
        if emit_logits:
            lg_ref = refs[pos]
            h_hi, h_lo = _split_bf16(h)
            w_hi, w_lo = _split_bf16(wr_ref[...])
            lg_ref[...] = (jnp.dot(h_hi, w_hi, preferred_element_type=F32)
                           + jnp.dot(h_lo, w_hi, preferred_element_type=F32)
                           + jnp.dot(h_hi, w_lo, preferred_element_type=F32))


def _resid_norm(x, y, gains, mod, la, ga_idx, gate_idx, nxt=None, h_dtype=BF16,
                w_router=None):
    B, S, D = x.shape
    ts = min(S, 256)
    row = pl.BlockSpec((None, ts, D), lambda b, i: (b, i, 0))
    in_specs = [row, row, _gain_spec(la, ga_idx, D), _mod_spec(la, gate_idx, D)]
    args = [x, y, gains, mod]
    out_specs = [row]
    out_shape = [jax.ShapeDtypeStruct((B, S, D), F32)]
    emit_h = nxt is not None
    emit_logits = w_router is not None
    if emit_h:
        lb, gb_idx, shift_idx, scale_idx = nxt
        in_specs += [_gain_spec(lb, gb_idx, D), _mod_spec(lb, scale_idx, D),
                     _mod_spec(lb, shift_idx, D)]
        args += [gains, mod, mod]
        out_specs.append(row)
        out_shape.append(jax.ShapeDtypeStruct((B, S, D), h_dtype))
    if emit_logits:
        in_specs.append(pl.BlockSpec((D, LANES), lambda b, i: (0, 0)))
        args.append(w_router)
        out_specs.append(pl.BlockSpec((None, ts, LANES), lambda b, i: (b, i, 0)))
        out_shape.append(jax.ShapeDtypeStruct((B, S, LANES), F32))
    outs = pl.pallas_call(
        functools.partial(_resid_norm_kernel, emit_h=emit_h, emit_logits=emit_logits),
        grid=(B, S // ts),
        in_specs=in_specs,
        out_specs=out_specs,
        out_shape=out_shape,
        compiler_params=_params(("arbitrary", "arbitrary"), 10 * ts * D * 4),
        name="resid_norm",
    )(*args)
    return outs


def _mm_kernel(a_ref, w_ref, o_ref):
    o_ref[...] = jnp.dot(a_ref[...], w_ref[...].astype(BF16),
                         preferred_element_type=F32).astype(o_ref.dtype)


def _nt_dot(a, wt):
    return lax.dot_general(a, wt, (((1,), (1,)), ((), ())), preferred_element_type=F32)


def _mm_nt_kernel(a_ref, wt_ref, o_ref):
    o_ref[...] = _nt_dot(a_ref[...], wt_ref[...].astype(BF16)).astype(o_ref.dtype)


def _qkv_kernel(a_ref, wt_ref, cs_ref, o_ref):
    acc = _nt_dot(a_ref[...], wt_ref[...].astype(BF16)) * cs_ref[...]
    for hh in range(o_ref.shape[0]):
        o_ref[hh] = acc[:, hh * HEAD_DIM:(hh + 1) * HEAD_DIM].astype(o_ref.dtype)


def _mm_two_kernel(a1_ref, a2_ref, w_ref, o_ref):
    k1 = a1_ref.shape[1]
    w = w_ref[...].astype(BF16)
    o_ref[...] = (jnp.dot(a1_ref[...], w[:k1], preferred_element_type=F32)
                  + jnp.dot(a2_ref[...], w[k1:], preferred_element_type=F32)).astype(o_ref.dtype)


def _gateup_kernel(a_ref, wg_ref, wu_ref, o_ref):
    a = a_ref[...]
    g = jnp.dot(a, wg_ref[...].astype(BF16), preferred_element_type=F32)
    u = jnp.dot(a, wu_ref[...].astype(BF16), preferred_element_type=F32)
    o_ref[...] = (g * jax.nn.sigmoid(g) * u).astype(o_ref.dtype)


def _w_spec(w, tn, lead, col_block_offset=0):
    K = w.shape[-2]
    nlead = len(lead)
    return pl.BlockSpec((None,) * nlead + (K, tn),
                        lambda i, j: tuple(lead) + (0, j + col_block_offset))


def _dense_vmem(tm, K, tn, n_w, w_bytes, out_bytes, a_buffers):
    a = a_buffers * tm * K * 2
    w = n_w * (2 * K * tn * w_bytes + K * tn * 2)
    o = 2 * tm * tn * out_bytes + n_w * tm * tn * 4
    return a + w + o


def _qkv_proj(a, wt, l, n_cols, tm, tn, colscale):
    M, K = a.shape
    hpb = tn // HEAD_DIM
    return pl.pallas_call(
        _qkv_kernel,
        grid=(M // tm, n_cols // tn),
        in_specs=[pl.BlockSpec((tm, K), lambda i, j: (i, 0)),
                  pl.BlockSpec((None, tn, K), lambda i, j: (l, j, 0)),
                  pl.BlockSpec((1, tn), lambda i, j: (0, j))],
        out_specs=pl.BlockSpec((hpb, tm, HEAD_DIM), lambda i, j: (j, i, 0)),
        out_shape=jax.ShapeDtypeStruct((n_cols // HEAD_DIM, M, HEAD_DIM), BF16),
        compiler_params=_params(("arbitrary", "arbitrary"),
                                _dense_vmem(tm, K, tn, 1, wt.dtype.itemsize, 2, 2)),
        name="in_proj_qkv",
    )(a, wt, colscale)


def _matmul_nt(a, wt, tm, tn, out_dtype, name):
    M, K = a.shape
    n_cols = wt.shape[0]
    return pl.pallas_call(
        _mm_nt_kernel,
        grid=(M // tm, n_cols // tn),
        in_specs=[pl.BlockSpec((tm, K), lambda i, j: (i, 0)),
                  pl.BlockSpec((tn, K), lambda i, j: (j, 0))],
        out_specs=pl.BlockSpec((tm, tn), lambda i, j: (i, j)),
        out_shape=jax.ShapeDtypeStruct((M, n_cols), out_dtype),
        compiler_params=_params(
            ("arbitrary", "arbitrary"),
            _dense_vmem(tm, K, tn, 1, wt.dtype.itemsize, jnp.dtype(out_dtype).itemsize, 2)),
        name=name,
    )(a, wt)


def _matmul_two(a1, a2, w, lead, tm, tn, out_dtype, name):
    M, K1 = a1.shape
    K2 = a2.shape[1]
    n_cols = w.shape[-1]
    return pl.pallas_call(
        _mm_two_kernel,
        grid=(M // tm, n_cols // tn),
        in_specs=[pl.BlockSpec((tm, K1), lambda i, j: (i, 0)),
                  pl.BlockSpec((tm, K2), lambda i, j: (i, 0)),
                  _w_spec(w, tn, lead)],
        out_specs=pl.BlockSpec((tm, tn), lambda i, j: (i, j)),
        out_shape=jax.ShapeDtypeStruct((M, n_cols), out_dtype),
        compiler_params=_params(
            ("arbitrary", "arbitrary"),
            _dense_vmem(tm, K1 + K2, tn, 1, w.dtype.itemsize, jnp.dtype(out_dtype).itemsize, 2)),
        name=name,
    )(a1, a2, w)


def _gateup(a, wg, wu, lead, tm, tn, name="ffn_gate_up"):
    M, K = a.shape
    n_cols = wg.shape[-1]
    return pl.pallas_call(
        _gateup_kernel,
        grid=(M // tm, n_cols // tn),
        in_specs=[pl.BlockSpec((tm, K), lambda i, j: (i, 0)),
                  _w_spec(wg, tn, lead), _w_spec(wu, tn, lead)],
        out_specs=pl.BlockSpec((tm, tn), lambda i, j: (i, j)),
        out_shape=jax.ShapeDtypeStruct((M, n_cols), BF16),
        compiler_params=_params(("arbitrary", "arbitrary"),
                                _dense_vmem(tm, K, tn, 2, wg.dtype.itemsize, 2, 2)),
        name=name,
    )(a, wg, wu)


def _down(a, w, lead, tm, tn, name="ffn_down"):
    M, K = a.shape
    n_cols = w.shape[-1]
    return pl.pallas_call(
        _mm_kernel,
        grid=(M // tm, n_cols // tn),
        in_specs=[pl.BlockSpec((tm, K), lambda i, j: (i, 0), pipeline_mode=pl.Buffered(1)),
                  _w_spec(w, tn, lead)],
        out_specs=pl.BlockSpec((tm, tn), lambda i, j: (i, j)),
        out_shape=jax.ShapeDtypeStruct((M, n_cols), F32),
        compiler_params=_params(("arbitrary", "arbitrary"),
                                _dense_vmem(tm, K, tn, 1, w.dtype.itemsize, 4, 1)),
        name=name,
    )(a, w)


def _fox_prep_kernel(f_ref, b_ref, fsh_ref, fhs_ref):
    z = f_ref[...] + b_ref[...]
    acc = jnp.minimum(z, 0.0) - jnp.log1p(jnp.exp(-jnp.abs(z)))
    S = acc.shape[0]
    row = lax.broadcasted_iota(I32, acc.shape, 0)
    d = 1
    while d < S:
        acc = acc + jnp.where(row >= d, pltpu.roll(acc, d, 0), 0.0)
        d *= 2
    fsh_ref[...] = acc
    fhs_ref[...] = acc.T


def _fox_prep(fu3, b_pad, col_block):
    B, S, _ = fu3.shape
    return pl.pallas_call(
        _fox_prep_kernel,
        grid=(B,),
        in_specs=[pl.BlockSpec((None, S, LANES), lambda b: (b, 0, col_block)),
                  pl.BlockSpec((1, LANES), lambda b: (0, 0))],
        out_specs=[pl.BlockSpec((None, S, LANES), lambda b: (b, 0, 0)),
                   pl.BlockSpec((None, LANES, S), lambda b: (b, 0, 0))],
        out_shape=[jax.ShapeDtypeStruct((B, S, LANES), F32),
                   jax.ShapeDtypeStruct((B, LANES, S), F32)],
        compiler_params=_params(("arbitrary",), 12 * S * LANES * 4),
        name="fox_prep",
    )(fu3, b_pad)


def _attn_kernel(q_ref, k_ref, v_ref, fq_ref, fk_ref, o_ref, fq_s, *, blk):
    h = pl.program_id(1)
    S = q_ref.shape[0]
    n_blk = S // blk
    n_chunk = blk // LANES
    lane = lax.broadcasted_iota(I32, fq_ref.shape, 1)
    fq_col = jnp.sum(jnp.where(lane == h, fq_ref[...], 0.0), axis=-1, keepdims=True)
    fq_s[...] = jnp.broadcast_to(fq_col, fq_s.shape)
    row_id = lax.broadcasted_iota(I32, (blk, LANES), 0)
    col_id = lax.broadcasted_iota(I32, (blk, LANES), 1)

    for i in range(n_blk):
        rows = slice(i * blk, (i + 1) * blk)
        q = q_ref[rows, :]
        fq = fq_s[rows, :]
        m_b = l_b = acc = None
        for j in range(i + 1):
            keys = slice(j * blk, (j + 1) * blk)
            s = lax.dot_general(q, k_ref[keys, :], (((1,), (1,)), ((), ())),
                                preferred_element_type=F32)
            chunks = []
            for c in range(n_chunk):
                lo = j * blk + c * LANES
                sc = s[:, c * LANES:(c + 1) * LANES] + (fq - fk_ref[:, lo:lo + LANES])
                if j == i:
                    sc = jnp.where(col_id + c * LANES <= row_id, sc, -jnp.inf)
                chunks.append(sc)
            m_cur = functools.reduce(jnp.maximum, chunks)
            m_row = jnp.max(m_cur, axis=-1, keepdims=True)
            m_new = m_row if m_b is None else jnp.maximum(m_b, m_row)
            m_new = jnp.broadcast_to(m_new, (blk, LANES))
            ps = [jnp.exp(sc - m_new) for sc in chunks]
            p_sum = functools.reduce(lambda a, b: a + b, ps)
            pv = jnp.dot(jnp.concatenate(ps, axis=1).astype(BF16), v_ref[keys, :],
                         preferred_element_type=F32)
            if m_b is None:
                l_b, acc = p_sum, pv
            else:
                alpha = jnp.exp(m_b - m_new)
                l_b = alpha * l_b + p_sum
                acc = alpha * acc + pv
            m_b = m_new
        l_row = jnp.sum(l_b, axis=-1, keepdims=True)
        o_ref[rows, :] = (acc / l_row).astype(o_ref.dtype)


def _attention(qkv_h, f_sh, f_hs4, n_heads):
    _, B, S, _ = qkv_h.shape
    H = n_heads
    blk = min(S, 512)
    head = lambda off: pl.BlockSpec((None, None, S, HEAD_DIM), lambda b, h: (off + h, b, 0, 0))
    return pl.pallas_call(
        functools.partial(_attn_kernel, blk=blk),
        grid=(B, H),
        in_specs=[head(0), head(H), head(2 * H),
                  pl.BlockSpec((None, S, LANES), lambda b, h: (b, 0, 0)),
                  pl.BlockSpec((None, None, 1, S), lambda b, h: (b, h, 0, 0))],
        out_specs=pl.BlockSpec((None, S, HEAD_DIM), lambda b, h: (b, 0, h)),
        out_shape=jax.ShapeDtypeStruct((B, S, H * HEAD_DIM), BF16),
        scratch_shapes=[pltpu.VMEM((S, LANES), F32)],
        compiler_params=_params(("arbitrary", "arbitrary"),
                                8 * S * HEAD_DIM * 2 + 3 * S * LANES * 4 + 8 * blk * blk * 4),
        name="fox_attention",
    )(qkv_h, qkv_h, qkv_h, f_sh, f_hs4)


def _pool_kernel(u_ref, w_ref, sc_ref, o_ref):
    g = pl.program_id(1)
    u = u_ref[...]
    row = lax.broadcasted_iota(I32, u.shape, 0)

    def shifted(v, d):
        return jnp.where(row >= d, pltpu.roll(v, d, 0), 0.0)

    s2 = u + shifted(u, 1)
    s4 = s2 + shifted(s2, 2)
    s8 = s4 + shifted(s4, 4)
    s16 = s8 + shifted(s8, 8)
    win = jnp.where(g == 0, s2, jnp.where(g == 1, s4, jnp.where(g == 2, s8, s16)))
    width = lax.shift_left(jnp.int32(POOL_WINDOWS[0]), g)
    count = jnp.minimum(row + 1, width).astype(F32)
    pooled = win / count - u
    y = jnp.dot(pooled.astype(BF16), w_ref[...].astype(BF16), preferred_element_type=F32)
    o_ref[...] = (y * sc_ref[...]).astype(o_ref.dtype)


def _pool(fu3, w_pool, pool_scale3, l):
    B, S, _ = fu3.shape
    G, C = w_pool.shape[1], w_pool.shape[2]
    assert G == len(POOL_WINDOWS) and all(
        w == POOL_WINDOWS[0] << k for k, w in enumerate(POOL_WINDOWS))
    return pl.pallas_call(
        _pool_kernel,
        grid=(B, G),
        in_specs=[pl.BlockSpec((None, S, C), lambda b, g: (b, 0, g)),
                  pl.BlockSpec((None, None, C, C), lambda b, g: (l, g, 0, 0)),
                  pl.BlockSpec((None, 1, C), lambda b, g: (l, 0, g))],
        out_specs=pl.BlockSpec((None, S, C), lambda b, g: (b, 0, g)),
        out_shape=jax.ShapeDtypeStruct((B, S, G * C), BF16),
        compiler_params=_params(("arbitrary", "arbitrary"), 14 * S * C * 4),
        name="pool_mixer",
    )(fu3, w_pool, pool_scale3)


def _route_kernel(lg_ref, dest_ref, wts_ref, meta_ref, m_all, m_top, tri, *,
                  n_experts, tile, chunk):
    N = lg_ref.shape[0]
    n_chunks = N // chunk
    lane = lax.broadcasted_iota(I32, (chunk, LANES), 1).astype(F32)
    lane1 = lax.broadcasted_iota(I32, (1, LANES), 1).astype(F32)
    r = lax.broadcasted_iota(I32, (chunk, chunk), 0)
    c = lax.broadcasted_iota(I32, (chunk, chunk), 1)
    tri[...] = jnp.where(c < r, 1.0, 0.0).astype(BF16)

    def select(ci, counts):
        rows = pl.ds(pl.multiple_of(ci * chunk, chunk), chunk)
        lg = jnp.where(lane < n_experts, lg_ref[rows, :], -jnp.inf)
        v1 = jnp.max(lg, axis=-1, keepdims=True)
        i1 = jnp.min(jnp.where(lg == v1, lane, float(LANES)), axis=-1, keepdims=True)
        first = lane == i1
        lg2 = jnp.where(first, -jnp.inf, lg)
        v2 = jnp.max(lg2, axis=-1, keepdims=True)
        i2 = jnp.min(jnp.where(lg2 == v2, lane, float(LANES)), axis=-1, keepdims=True)
        second = lane == i2
        e2 = jnp.exp(v2 - v1)
        w1 = 1.0 / (1.0 + e2)
        w2 = e2 / (1.0 + e2)
        wts_ref[rows, :] = jnp.where(lane == 0.0, w1, jnp.where(lane == 1.0, w2, 0.0))
        both = jnp.where(first | second, 1.0, 0.0)
        m_all[rows, :] = both.astype(BF16)
        m_top[rows, :] = jnp.where(first, 1.0, 0.0).astype(BF16)
        return counts + jnp.sum(both, axis=0, keepdims=True)

    counts = lax.fori_loop(0, n_chunks, select, jnp.zeros((1, LANES), F32))
    padded = jnp.ceil(counts / tile) * tile
    start = jnp.zeros((1, LANES), F32)
    tile_row = lane1 * tile
    tile_expert = jnp.zeros((1, LANES), F32)
    last_used = jnp.zeros((1, 1), F32)
    running = jnp.zeros((1, 1), F32)
    for e in range(n_experts):
        rows_e = jnp.sum(jnp.where(lane1 == e, padded, 0.0), axis=-1, keepdims=True)
        start = start + jnp.where(lane1 > e, rows_e, 0.0)
        running = running + rows_e
        tile_expert = tile_expert + jnp.where(tile_row >= running, 1.0, 0.0)
        last_used = jnp.where(rows_e > 0.0, float(e), last_used)
    tile_expert = jnp.minimum(tile_expert, last_used)
    n_tiles = running / tile
    sub = lax.broadcasted_iota(I32, (SUBLANES, LANES), 0)
    meta_ref[...] = jnp.where(sub == 0, tile_expert, jnp.where(sub == 1, n_tiles, 0.0)).astype(I32)

    def place(ci, carry):
        rows = pl.ds(pl.multiple_of(ci * chunk, chunk), chunk)
        both = m_all[rows, :]
        top = m_top[rows, :].astype(F32)
        rank = jnp.dot(tri[...], both, preferred_element_type=F32)
        pos = start + carry + rank
        both_f = both.astype(F32)
        d1 = jnp.sum(top * pos, axis=-1, keepdims=True)
        d2 = jnp.sum((both_f - top) * pos, axis=-1, keepdims=True)
        dest_ref[rows, :] = jnp.where(lane == 0.0, d1, jnp.where(lane == 1.0, d2, 0.0)).astype(I32)
        return carry + jnp.sum(both_f, axis=0, keepdims=True)

    lax.fori_loop(0, n_chunks, place, jnp.zeros((1, LANES), F32))


def _route(logits, n_experts, tile):
    N = logits.shape[0]
    chunk = min(N, 512)
    return pl.pallas_call(
        functools.partial(_route_kernel, n_experts=n_experts, tile=tile, chunk=chunk),
        out_shape=[jax.ShapeDtypeStruct((N, LANES), I32),
                   jax.ShapeDtypeStruct((N, LANES), F32),
                   jax.ShapeDtypeStruct((SUBLANES, LANES), I32)],
        scratch_shapes=[pltpu.VMEM((N, LANES), BF16), pltpu.VMEM((N, LANES), BF16),
                        pltpu.VMEM((chunk, chunk), BF16)],
        compiler_params=pltpu.CompilerParams(
            vmem_limit_bytes=int(min(max(N * LANES * 4 * 8, 16 * 2**20), VMEM_CAP_BYTES))),
        name="moe_route",
    )(logits)


def _gather_kernel(dest_ref, nt_ref, h_hbm, o_ref, src, stage, sems, *, tile, n_rows, n_assign):
    t = pl.program_id(0)
    n_used = nt_ref[0]
    slot = t % 2

    def issue_tile(tt, to_slot):
        base = tt * tile

        def issue(rr, carry):
            tok = src[base + rr]
            pltpu.make_async_copy(h_hbm.at[pl.ds(tok, 1), :],
                                  stage.at[to_slot, pl.ds(rr, 1), :], sems.at[to_slot]).start()
            return carry
        lax.fori_loop(0, tile, issue, 0, unroll=8)

    @pl.when(t == 0)
    def _invert_dest():
        def clear(rr, carry):
            src[rr] = 0
            return carry
        lax.fori_loop(0, n_rows, clear, 0, unroll=8)

        def scatter(a, carry):
            src[dest_ref[a]] = a // TOP_K
            return carry
        lax.fori_loop(0, n_assign, scatter, 0, unroll=8)
        issue_tile(0, 0)

    @pl.when(t + 1 < n_used)
    def _prefetch_next():
        issue_tile(t + 1, 1 - slot)

    @pl.when(t < n_used)
    def _gather():
        pltpu.make_async_copy(h_hbm.at[pl.ds(0, tile), :], stage.at[slot], sems.at[slot]).wait()
        o_ref[...] = stage[slot].astype(o_ref.dtype)

    @pl.when(t >= n_used)
    def _unused_tile():
        o_ref[...] = jnp.zeros(o_ref.shape, o_ref.dtype)


def _gather(dest_flat, n_tiles, h2, tile, max_tiles):
    N, D = h2.shape
    n_rows = max_tiles * tile
    grid_spec = pltpu.PrefetchScalarGridSpec(
        num_scalar_prefetch=2,
        grid=(max_tiles,),
        in_specs=[pl.BlockSpec(memory_space=pl.ANY)],
        out_specs=pl.BlockSpec((tile, D), lambda t, dest, nt: (t, 0)),
        scratch_shapes=[pltpu.SMEM((n_rows,), I32), pltpu.VMEM((2, tile, D), F32),
                        pltpu.SemaphoreType.DMA((2,))],
    )
    return pl.pallas_call(
        functools.partial(_gather_kernel, tile=tile, n_rows=n_rows, n_assign=dest_flat.shape[0]),
        grid_spec=grid_spec,
        out_shape=jax.ShapeDtypeStruct((n_rows, D), BF16),
        compiler_params=_params(("arbitrary",), 4 * tile * D * 4),
        name="moe_gather",
    )(dest_flat, n_tiles, h2)


def _grouped_gateup_kernel(te_ref, nt_ref, a_ref, wg_ref, wu_ref, o_ref):
    t = pl.program_id(1)

    @pl.when(t < nt_ref[0])
    def _compute():
        _gateup_kernel(a_ref, wg_ref, wu_ref, o_ref)

    @pl.when(t >= nt_ref[0])
    def _unused_tile():
        o_ref[...] = jnp.zeros(o_ref.shape, o_ref.dtype)


def _grouped_down_kernel(te_ref, nt_ref, a_ref, w_ref, o_ref):
    t = pl.program_id(1)

    @pl.when(t < nt_ref[0])
    def _compute():
        _mm_kernel(a_ref, w_ref, o_ref)

    @pl.when(t >= nt_ref[0])
    def _unused_tile():
        o_ref[...] = jnp.zeros(o_ref.shape, o_ref.dtype)


def _grouped(body, a, ws, l, tile_expert, n_tiles, tile, tn, out_dtype, name):
    R, K = a.shape
    n_cols = ws[0].shape[-1]
    max_tiles = R // tile

    def used(t, nt):
        return jnp.minimum(t, nt[0] - 1)

    a_spec = pl.BlockSpec((tile, K), lambda j, t, te, nt: (used(t, nt), 0))
    w_spec = pl.BlockSpec((None, None, K, tn),
                          lambda j, t, te, nt: (l, te[used(t, nt)], 0, j))
    grid_spec = pltpu.PrefetchScalarGridSpec(
        num_scalar_prefetch=2,
        grid=(n_cols // tn, max_tiles),
        in_specs=[a_spec] + [w_spec] * len(ws),
        out_specs=pl.BlockSpec((tile, tn), lambda j, t, te, nt: (t, j)),
    )
    return pl.pallas_call(
        body,
        grid_spec=grid_spec,
        out_shape=jax.ShapeDtypeStruct((R, n_cols), out_dtype),
        compiler_params=_params(
            ("arbitrary", "arbitrary"),
            _dense_vmem(tile, K, tn, len(ws), ws[0].dtype.itemsize,
                        jnp.dtype(out_dtype).itemsize, 2)),
        name=name,
    )(tile_expert, n_tiles, a, *ws)


def _combine_kernel(dest_ref, ys_hbm, wts_ref, x_ref, g_ref, gate_ref, o_ref,
                    stage, sems, *, tile, tiles_per_seq, n_steps):
    t = pl.program_id(0) * tiles_per_seq + pl.program_id(1)
    slot = t % 2

    def issue_tile(tt, to_slot):
        n0 = tt * tile

        def issue(rr, carry):
            a = (n0 + rr) * TOP_K
            for k in range(TOP_K):
                pltpu.make_async_copy(ys_hbm.at[pl.ds(dest_ref[a + k], 1), :],
                                      stage.at[to_slot, k, pl.ds(rr, 1), :],
                                      sems.at[to_slot, k]).start()
            return carry
        lax.fori_loop(0, tile, issue, 0, unroll=4)

    @pl.when(t == 0)
    def _first_tile():
        issue_tile(0, 0)

    @pl.when(t + 1 < n_steps)
    def _prefetch_next():
        issue_tile(t + 1, 1 - slot)

    for k in range(TOP_K):
        pltpu.make_async_copy(ys_hbm.at[pl.ds(0, tile), :], stage.at[slot, k],
                              sems.at[slot, k]).wait()
    w = wts_ref[...]
    y = w[:, 0:1] * stage[slot, 0] + w[:, 1:2] * stage[slot, 1]
    o_ref[...] = x_ref[...] + gate_ref[...] * _rms(y, g_ref[...])


def _combine(dest_flat, ys, wts3, x, gains, mod, l, g_idx, gate_idx):
    B, S, D = x.shape
    tile = min(S, 256)
    row = pl.BlockSpec((None, tile, D), lambda b, i, dest: (b, i, 0))
    grid_spec = pltpu.PrefetchScalarGridSpec(
        num_scalar_prefetch=1,
        grid=(B, S // tile),
        in_specs=[pl.BlockSpec(memory_space=pl.ANY),
                  pl.BlockSpec((None, tile, LANES), lambda b, i, dest: (b, i, 0)),
                  row,
                  pl.BlockSpec((None, None, 1, D), lambda b, i, dest: (l, g_idx, 0, 0)),
                  pl.BlockSpec((None, None, None, 1, D),
                               lambda b, i, dest: (l, b, gate_idx, 0, 0))],
        out_specs=row,
        scratch_shapes=[pltpu.VMEM((2, TOP_K, tile, D), F32),
                        pltpu.SemaphoreType.DMA((2, TOP_K))],
    )
    return pl.pallas_call(
        functools.partial(_combine_kernel, tile=tile, tiles_per_seq=S // tile,
                          n_steps=B * (S // tile)),
        grid_spec=grid_spec,
        out_shape=jax.ShapeDtypeStruct((B, S, D), F32),
        compiler_params=_params(("arbitrary", "arbitrary"), 12 * tile * D * 4),
        name="moe_combine",
    )(dest_flat, ys, wts3, x, gains, mod)


def kernel(x, c, norm_gains, w_mod, b_mod, w_in, b_forget, w_pool, pool_scale, w_out,
           w_ffn_gate, w_ffn_up, w_ffn_down, w_router, w_exp_gate, w_exp_up, w_exp_down):
    B, S, D = x.shape
    L = w_mod.shape[0]
    N = B * S
    H = b_forget.shape[1]
    A = H * HEAD_DIM
    P = pool_scale.shape[1]
    assert A + P == D and w_in.shape[2] == 3 * A + H + P and H <= LANES
    assert P % LANES == 0 and A % LANES == 0
    E = w_router.shape[2]
    tm = min(N, 1024)

    mod = _mod(c, w_mod, b_mod)
    gains = norm_gains.reshape(L, 4, 1, D)
    h = _norm_mod(x, gains, mod, 0, 0, 0, 1, BF16)

    qk_scale = jnp.concatenate([jnp.full((1, A), HEAD_DIM ** -0.5, F32),
                                jnp.ones((1, 2 * A), F32)], axis=1)
    pool_scale3 = pool_scale.reshape(L, 1, P)
    w_in_t = jnp.swapaxes(w_in, 1, 2)

    for l in range(L):
        h2 = h.reshape(N, D)
        qkv_h = _qkv_proj(h2, w_in_t, l, 3 * A, tm, min(3 * A, 512), qk_scale)
        w_fu_t = jnp.concatenate([w_in_t[l, 3 * A + H:], w_in_t[l, 3 * A:3 * A + H],
                                  jnp.zeros((LANES - H, D), F32)], axis=0)
        tn_fu = LANES * 3 if (P + LANES) % (LANES * 3) == 0 else LANES
        fu = _matmul_nt(h2, w_fu_t, tm, tn_fu, F32, "in_proj_gate_pool")
        fu3 = fu.reshape(B, S, P + LANES)
        b_pad = jnp.pad(b_forget[l], (0, LANES - H)).reshape(1, LANES)
        f_sh, f_hs = _fox_prep(fu3, b_pad, P // LANES)
        attn = _attention(qkv_h.reshape(3 * H, B, S, HEAD_DIM), f_sh,
                          f_hs.reshape(B, LANES, 1, S), H)
        pooled = _pool(fu3, w_pool, pool_scale3, l)
        mixed = _matmul_two(attn.reshape(N, A), pooled.reshape(N, P), w_out, (l,), tm,
                            min(D, 512), F32, "out_proj")

        i = l // 2
        if l % 2 == 0:
            x, h = _resid_norm(x, mixed.reshape(B, S, D), gains, mod, l, 1, 2,
                               nxt=(l, 2, 3, 4))
            FF = w_ffn_gate.shape[2]
            tn_ff = 256 if FF % 256 == 0 else LANES
            act = _gateup(h.reshape(N, D), w_ffn_gate, w_ffn_up, (i,), tm, tn_ff)
            y = _down(act, w_ffn_down, (i,), tm, min(D, 256))
            y3 = y.reshape(B, S, D)
            if l + 1 < L:
                x, h = _resid_norm(x, y3, gains, mod, l, 3, 5, nxt=(l + 1, 0, 0, 1))
            else:
                (x,) = _resid_norm(x, y3, gains, mod, l, 3, 5)
        else:
            wr_pad = jnp.pad(w_router[i], ((0, 0), (0, LANES - E)))
            x, hf, logits = _resid_norm(x, mixed.reshape(B, S, D), gains, mod, l, 1, 2,
                                        nxt=(l, 2, 3, 4), h_dtype=F32, w_router=wr_pad)
            tile = min(N, 512)
            max_tiles = (N * TOP_K) // tile + E
            assert max_tiles <= LANES
            dest, wts, meta = _route(logits.reshape(N, LANES), E, tile)
            dest_flat = dest[:, :TOP_K].reshape(N * TOP_K)
            tile_expert = meta[0, :max_tiles]
            n_tiles = meta[1, :1]
            xs = _gather(dest_flat, n_tiles, hf.reshape(N, D), tile, max_tiles)
            tn_e = min(w_exp_gate.shape[3], 512)
            act = _grouped(_grouped_gateup_kernel, xs, (w_exp_gate, w_exp_up), i,
                           tile_expert, n_tiles, tile, tn_e, BF16, "moe_gate_up")
            ys = _grouped(_grouped_down_kernel, act, (w_exp_down,), i,
                          tile_expert, n_tiles, tile, min(D, 512), F32, "moe_down")
            x = _combine(dest_flat, ys, wts.reshape(B, S, LANES), x, gains, mod, l, 3, 5)
            if l + 1 < L:
                h = _norm_mod(x, gains, mod, l + 1, 0, 0, 1, BF16)
    return x
```

```python
import functools

import jax
import jax.numpy as jnp
from jax import lax
from jax.experimental import pallas as pl
from jax.experimental.pallas import tpu as pltpu

F32 = jnp.float32
BF16 = jnp.bfloat16
I32 = jnp.int32

HEAD_DIM = 128
POOL_WINDOWS = (2, 4, 8, 16)
N_MOD = 6
TOP_K = 2
RMS_EPS = 1e-6
LOG2_E = 1.4426950408889634

LANES = 128
SUBLANES = 8
V7X_VMEM_BYTES = 64 * 2**20
VMEM_CAP_BYTES = V7X_VMEM_BYTES - 6 * 2**20


def _params(semantics, vmem_bytes):
    limit = int(min(max(vmem_bytes * 5 // 4, 16 * 2**20), VMEM_CAP_BYTES))
    return pltpu.CompilerParams(dimension_semantics=semantics, vmem_limit_bytes=limit)


def _rms(x, gain):
    inv = lax.rsqrt(jnp.mean(x * x, axis=-1, keepdims=True) + RMS_EPS)
    return x * inv * gain


def _mod_kernel(c_ref, w_ref, b_ref, o_ref):
    c = c_ref[...]
    c_act = (c * jax.nn.sigmoid(c)).astype(BF16)
    o_ref[...] = jnp.dot(c_act, w_ref[...].astype(BF16),
                         preferred_element_type=F32) + b_ref[...]


def _mod(c, w_mod, b_mod):
    B, D = c.shape
    L, _, NM = w_mod.shape
    rows = -(-B // SUBLANES) * SUBLANES
    c_pad = jnp.pad(c, ((0, rows - B), (0, 0)))
    tn = 512
    out = pl.pallas_call(
        _mod_kernel,
        grid=(L, NM // tn),
        in_specs=[
            pl.BlockSpec((rows, D), lambda l, j: (0, 0)),
            pl.BlockSpec((None, D, tn), lambda l, j: (l, 0, j)),
            pl.BlockSpec((None, 1, tn), lambda l, j: (l, 0, j)),
        ],
        out_specs=pl.BlockSpec((None, rows, tn), lambda l, j: (l, 0, j)),
        out_shape=jax.ShapeDtypeStruct((L, rows, NM), F32),
        compiler_params=_params(("arbitrary", "arbitrary"), 3 * D * tn * 4),
        name="adaln_mod",
    )(c_pad, w_mod, b_mod.reshape(L, 1, NM))
    return out[:, :B].reshape(L, B, N_MOD, 1, D)


def _mod_spec(l, idx, D):
    return pl.BlockSpec((None, None, None, 1, D), lambda b, i: (l, b, idx, 0, 0))


def _gain_spec(l, idx, D):
    return pl.BlockSpec((None, None, 1, D), lambda b, i: (l, idx, 0, 0))


def _norm_mod_kernel(x_ref, g_ref, sc_ref, sh_ref, o_ref):
    h = _rms(x_ref[...], g_ref[...])
    o_ref[...] = (h * (1.0 + sc_ref[...]) + sh_ref[...]).astype(o_ref.dtype)


def _norm_mod(x, gains, mod, l, gain_idx, shift_idx, scale_idx, out_dtype):
    B, S, D = x.shape
    ts = min(S, 256)
    row = pl.BlockSpec((None, ts, D), lambda b, i: (b, i, 0))
    return pl.pallas_call(
        _norm_mod_kernel,
        grid=(B, S // ts),
        in_specs=[row, _gain_spec(l, gain_idx, D), _mod_spec(l, scale_idx, D),
                  _mod_spec(l, shift_idx, D)],
        out_specs=row,
        out_shape=jax.ShapeDtypeStruct((B, S, D), out_dtype),
        compiler_params=_params(("arbitrary", "arbitrary"), 4 * ts * D * 4),
        name="pre_norm",
    )(x, gains, mod, mod)


def _split_bf16(v):
    hi = v.astype(BF16)
    lo = (v - hi.astype(F32)).astype(BF16)
    return hi, lo


def _resid_norm_kernel(*refs, emit_h, emit_logits):
    x_ref, y_ref, ga_ref, gate_ref = refs[:4]
    pos = 4
    if emit_h:
        gb_ref, sc_ref, sh_ref = refs[pos:pos + 3]
        pos += 3
    if emit_logits:
        wr_ref = refs[pos]
        pos += 1
    xo_ref = refs[pos]
    pos += 1
    x_new = x_ref[...] + gate_ref[...] * _rms(y_ref[...], ga_ref[...])
    xo_ref[...] = x_new
    if emit_h:
        h_ref = refs[pos]
        pos += 1
        h = _rms(x_new, gb_ref[...]) * (1.0 + sc_ref[...]) + sh_ref[...]
        h_ref[...] = h.astype(h_ref.dtype)
        if emit_logits:
            lg_ref = refs[pos]
            h_hi, h_lo = _split_bf16(h)
            w_hi, w_lo = _split_bf16(wr_ref[...])
            lg_ref[...] = (jnp.dot(h_hi, w_hi, preferred_element_type=F32)
                           + jnp.dot(h_lo, w_hi, preferred_element_type=F32)
                           + jnp.dot(h_hi, w_lo, preferred_element_type=F32))


def _resid_norm(x, y, gains, mod, la, ga_idx, gate_idx, nxt=None, h_dtype=BF16,
                w_router=None):
    B, S, D = x.shape
    ts = min(S, 256)
    row = pl.BlockSpec((None, ts, D), lambda b, i: (b, i, 0))
    in_specs = [row, row, _gain_spec(la, ga_idx, D), _mod_spec(la, gate_idx, D)]
    args = [x, y, gains, mod]
    out_specs = [row]
    out_shape = [jax.ShapeDtypeStruct((B, S, D), F32)]
    emit_h = nxt is not None
    emit_logits = w_router is not None
    if emit_h:
        lb, gb_idx, shift_idx, scale_idx = nxt
        in_specs += [_gain_spec(lb, gb_idx, D), _mod_spec(lb, scale_idx, D),
                     _mod_spec(lb, shift_idx, D)]
        args += [gains, mod, mod]
        out_specs.append(row)
        out_shape.append(jax.ShapeDtypeStruct((B, S, D), h_dtype))
    if emit_logits:
        in_specs.append(pl.BlockSpec((D, LANES), lambda b, i: (0, 0)))
        args.append(w_router)
        out_specs.append(pl.BlockSpec((None, ts, LANES), lambda b, i: (b, i, 0)))
        out_shape.append(jax.ShapeDtypeStruct((B, S, LANES), F32))
    outs = pl.pallas_call(
        functools.partial(_resid_norm_kernel, emit_h=emit_h, emit_logits=emit_logits),
        grid=(B, S // ts),
        in_specs=in_specs,
        out_specs=out_specs,
        out_shape=out_shape,
        compiler_params=_params(("arbitrary", "arbitrary"), 10 * ts * D * 4),
        name="resid_norm",
    )(*args)
    return outs


def _mm_kernel(a_ref, w_ref, o_ref):
    o_ref[...] = jnp.dot(a_ref[...], w_ref[...].astype(BF16),
                         preferred_element_type=F32).astype(o_ref.dtype)


def _nt_dot(a, wt):
    return lax.dot_general(a, wt, (((1,), (1,)), ((), ())), preferred_element_type=F32)


def _mm_nt_kernel(a_ref, wt_ref, o_ref):
    o_ref[...] = _nt_dot(a_ref[...], wt_ref[...].astype(BF16)).astype(o_ref.dtype)


def _qkv_kernel(a_ref, wt_ref, cs_ref, o_ref):
    acc = _nt_dot(a_ref[...], wt_ref[...].astype(BF16)) * cs_ref[...]
    for hh in range(o_ref.shape[0]):
        o_ref[hh] = acc[:, hh * HEAD_DIM:(hh + 1) * HEAD_DIM].astype(o_ref.dtype)


def _mm_two_kernel(a1_ref, a2_ref, w_ref, o_ref):
    k1 = a1_ref.shape[1]
    w = w_ref[...].astype(BF16)
    o_ref[...] = (jnp.dot(a1_ref[...], w[:k1], preferred_element_type=F32)
                  + jnp.dot(a2_ref[...], w[k1:], preferred_element_type=F32)).astype(o_ref.dtype)


def _gateup_kernel(a_ref, wg_ref, wu_ref, o_ref):
    a = a_ref[...]
    g = jnp.dot(a, wg_ref[...].astype(BF16), preferred_element_type=F32)
    u = jnp.dot(a, wu_ref[...].astype(BF16), preferred_element_type=F32)
    o_ref[...] = (g * jax.nn.sigmoid(g) * u).astype(o_ref.dtype)


def _w_spec(w, tn, lead, col_block_offset=0):
    K = w.shape[-2]
    nlead = len(lead)
    return pl.BlockSpec((None,) * nlead + (K, tn),
                        lambda i, j: tuple(lead) + (0, j + col_block_offset))


def _a_spec(tm, K):
    return pl.BlockSpec((tm, K), lambda i, j: (i, 0), pipeline_mode=pl.Buffered(1))


def _dense_vmem(tm, K, tn, n_w, w_bytes, out_bytes, a_buffers):
    a = a_buffers * tm * K * 2
    w = n_w * (2 * K * tn * w_bytes + K * tn * 2)
    o = 2 * tm * tn * out_bytes + n_w * tm * tn * 4
    return a + w + o


def _qkv_proj(a, wt, l, n_cols, tm, tn, colscale):
    M, K = a.shape
    hpb = tn // HEAD_DIM
    return pl.pallas_call(
        _qkv_kernel,
        grid=(M // tm, n_cols // tn),
        in_specs=[_a_spec(tm, K),
                  pl.BlockSpec((None, tn, K), lambda i, j: (l, j, 0)),
                  pl.BlockSpec((1, tn), lambda i, j: (0, j))],
        out_specs=pl.BlockSpec((hpb, tm, HEAD_DIM), lambda i, j: (j, i, 0)),
        out_shape=jax.ShapeDtypeStruct((n_cols // HEAD_DIM, M, HEAD_DIM), BF16),
        compiler_params=_params(("arbitrary", "arbitrary"),
                                _dense_vmem(tm, K, tn, 1, wt.dtype.itemsize, 2, 1)),
        name="in_proj_qkv",
    )(a, wt, colscale)


def _matmul_nt(a, wt, tm, tn, out_dtype, name):
    M, K = a.shape
    n_cols = wt.shape[0]
    return pl.pallas_call(
        _mm_nt_kernel,
        grid=(M // tm, n_cols // tn),
        in_specs=[_a_spec(tm, K),
                  pl.BlockSpec((tn, K), lambda i, j: (j, 0))],
        out_specs=pl.BlockSpec((tm, tn), lambda i, j: (i, j)),
        out_shape=jax.ShapeDtypeStruct((M, n_cols), out_dtype),
        compiler_params=_params(
            ("arbitrary", "arbitrary"),
            _dense_vmem(tm, K, tn, 1, wt.dtype.itemsize, jnp.dtype(out_dtype).itemsize, 1)),
        name=name,
    )(a, wt)


def _matmul_two(a1, a2, w, lead, tm, tn, out_dtype, name):
    M, K1 = a1.shape
    K2 = a2.shape[1]
    n_cols = w.shape[-1]
    return pl.pallas_call(
        _mm_two_kernel,
        grid=(M // tm, n_cols // tn),
        in_specs=[_a_spec(tm, K1), _a_spec(tm, K2), _w_spec(w, tn, lead)],
        out_specs=pl.BlockSpec((tm, tn), lambda i, j: (i, j)),
        out_shape=jax.ShapeDtypeStruct((M, n_cols), out_dtype),
        compiler_params=_params(
            ("arbitrary", "arbitrary"),
            _dense_vmem(tm, K1 + K2, tn, 1, w.dtype.itemsize, jnp.dtype(out_dtype).itemsize, 1)),
        name=name,
    )(a1, a2, w)


def _gateup(a, wg, wu, lead, tm, tn, name="ffn_gate_up"):
    M, K = a.shape
    n_cols = wg.shape[-1]
    return pl.pallas_call(
        _gateup_kernel,
        grid=(M // tm, n_cols // tn),
        in_specs=[_a_spec(tm, K), _w_spec(wg, tn, lead), _w_spec(wu, tn, lead)],
        out_specs=pl.BlockSpec((tm, tn), lambda i, j: (i, j)),
        out_shape=jax.ShapeDtypeStruct((M, n_cols), BF16),
        compiler_params=_params(("arbitrary", "arbitrary"),
                                _dense_vmem(tm, K, tn, 2, wg.dtype.itemsize, 2, 1)),
        name=name,
    )(a, wg, wu)


def _down(a, w, lead, tm, tn, name="ffn_down"):
    M, K = a.shape
    n_cols = w.shape[-1]
    return pl.pallas_call(
        _mm_kernel,
        grid=(M // tm, n_cols // tn),
        in_specs=[_a_spec(tm, K), _w_spec(w, tn, lead)],
        out_specs=pl.BlockSpec((tm, tn), lambda i, j: (i, j)),
        out_shape=jax.ShapeDtypeStruct((M, n_cols), F32),
        compiler_params=_params(("arbitrary", "arbitrary"),
                                _dense_vmem(tm, K, tn, 1, w.dtype.itemsize, 4, 1)),
        name=name,
    )(a, w)


def _fox_prep_kernel(f_ref, b_ref, fsh_ref, fhs_ref):
    z = f_ref[...] + b_ref[...]
    acc = jnp.minimum(z, 0.0) - jnp.log1p(jnp.exp(-jnp.abs(z)))
    S = acc.shape[0]
    row = lax.broadcasted_iota(I32, acc.shape, 0)
    d = 1
    while d < S:
        acc = acc + jnp.where(row >= d, pltpu.roll(acc, d, 0), 0.0)
        d *= 2
    acc = acc * LOG2_E
    fsh_ref[...] = acc
    fhs_ref[...] = acc.T


def _fox_prep(fu3, b_pad, col_block):
    B, S, _ = fu3.shape
    return pl.pallas_call(
        _fox_prep_kernel,
        grid=(B,),
        in_specs=[pl.BlockSpec((None, S, LANES), lambda b: (b, 0, col_block)),
                  pl.BlockSpec((1, LANES), lambda b: (0, 0))],
        out_specs=[pl.BlockSpec((None, S, LANES), lambda b: (b, 0, 0)),
                   pl.BlockSpec((None, LANES, S), lambda b: (b, 0, 0))],
        out_shape=[jax.ShapeDtypeStruct((B, S, LANES), F32),
                   jax.ShapeDtypeStruct((B, LANES, S), F32)],
        compiler_params=_params(("arbitrary",), 12 * S * LANES * 4),
        name="fox_prep",
    )(fu3, b_pad)


def _attn_kernel(q_ref, k_ref, v_ref, fq_ref, fk_ref, o_ref, fq_s, *, blk):
    h = pl.program_id(1)
    S = q_ref.shape[0]
    n_blk = S // blk
    n_chunk = blk // LANES
    lane = lax.broadcasted_iota(I32, fq_ref.shape, 1)
    fq_col = jnp.sum(jnp.where(lane == h, fq_ref[...], 0.0), axis=-1, keepdims=True)
    fq_s[...] = jnp.broadcast_to(fq_col, fq_s.shape)
    row_id = lax.broadcasted_iota(I32, (blk, LANES), 0)
    col_id = lax.broadcasted_iota(I32, (blk, LANES), 1)

    for i in range(n_blk):
        rows = slice(i * blk, (i + 1) * blk)
        q = q_ref[rows, :]
        fq = fq_s[rows, :]
        m_b = l_b = acc = None
        for j in range(i + 1):
            keys = slice(j * blk, (j + 1) * blk)
            s = lax.dot_general(q, k_ref[keys, :], (((1,), (1,)), ((), ())),
                                preferred_element_type=F32)
            chunks = []
            for c in range(n_chunk):
                lo = j * blk + c * LANES
                sc = s[:, c * LANES:(c + 1) * LANES] + (fq - fk_ref[:, lo:lo + LANES])
                if j == i:
                    sc = jnp.where(col_id + c * LANES <= row_id, sc, -jnp.inf)
                chunks.append(sc)
            m_cur = functools.reduce(jnp.maximum, chunks)
            m_row = jnp.max(m_cur, axis=-1, keepdims=True)
            m_new = m_row if m_b is None else jnp.maximum(m_b, m_row)
            m_new = jnp.broadcast_to(m_new, (blk, LANES))
            ps = [jnp.exp2(sc - m_new) for sc in chunks]
            p_sum = functools.reduce(lambda a, b: a + b, ps)
            pv = jnp.dot(jnp.concatenate(ps, axis=1).astype(BF16), v_ref[keys, :],
                         preferred_element_type=F32)
            if m_b is None:
                l_b, acc = p_sum, pv
            else:
                alpha = jnp.exp2(m_b - m_new)
                l_b = alpha * l_b + p_sum
                acc = alpha * acc + pv
            m_b = m_new
        l_row = jnp.sum(l_b, axis=-1, keepdims=True)
        o_ref[rows, :] = (acc / l_row).astype(o_ref.dtype)


def _attention(qkv_h, f_sh, f_hs4, n_heads):
    _, B, S, _ = qkv_h.shape
    H = n_heads
    blk = min(S, 512)
    head = lambda off: pl.BlockSpec((None, None, S, HEAD_DIM), lambda b, h: (off + h, b, 0, 0))
    return pl.pallas_call(
        functools.partial(_attn_kernel, blk=blk),
        grid=(B, H),
        in_specs=[head(0), head(H), head(2 * H),
                  pl.BlockSpec((None, S, LANES), lambda b, h: (b, 0, 0)),
                  pl.BlockSpec((None, None, 1, S), lambda b, h: (b, h, 0, 0))],
        out_specs=pl.BlockSpec((None, S, HEAD_DIM), lambda b, h: (b, 0, h)),
        out_shape=jax.ShapeDtypeStruct((B, S, H * HEAD_DIM), BF16),
        scratch_shapes=[pltpu.VMEM((S, LANES), F32)],
        compiler_params=_params(("arbitrary", "arbitrary"),
                                8 * S * HEAD_DIM * 2 + 3 * S * LANES * 4 + 8 * blk * blk * 4),
        name="fox_attention",
    )(qkv_h, qkv_h, qkv_h, f_sh, f_hs4)


def _pool_kernel(u_ref, w_ref, sc_ref, o_ref):
    g = pl.program_id(1)
    u = u_ref[...]
    row = lax.broadcasted_iota(I32, u.shape, 0)

    def shifted(v, d):
        return jnp.where(row >= d, pltpu.roll(v, d, 0), 0.0)

    s2 = u + shifted(u, 1)
    s4 = s2 + shifted(s2, 2)
    s8 = s4 + shifted(s4, 4)
    s16 = s8 + shifted(s8, 8)
    win = jnp.where(g == 0, s2, jnp.where(g == 1, s4, jnp.where(g == 2, s8, s16)))
    width = lax.shift_left(jnp.int32(POOL_WINDOWS[0]), g)
    count = jnp.minimum(row + 1, width).astype(F32)
    pooled = win / count - u
    y = jnp.dot(pooled.astype(BF16), w_ref[...].astype(BF16), preferred_element_type=F32)
    o_ref[...] = (y * sc_ref[...]).astype(o_ref.dtype)


def _pool(fu3, w_pool, pool_scale3, l):
    B, S, _ = fu3.shape
    G, C = w_pool.shape[1], w_pool.shape[2]
    assert G == len(POOL_WINDOWS) and all(
        w == POOL_WINDOWS[0] << k for k, w in enumerate(POOL_WINDOWS))
    return pl.pallas_call(
        _pool_kernel,
        grid=(B, G),
        in_specs=[pl.BlockSpec((None, S, C), lambda b, g: (b, 0, g)),
                  pl.BlockSpec((None, None, C, C), lambda b, g: (l, g, 0, 0)),
                  pl.BlockSpec((None, 1, C), lambda b, g: (l, 0, g))],
        out_specs=pl.BlockSpec((None, S, C), lambda b, g: (b, 0, g)),
        out_shape=jax.ShapeDtypeStruct((B, S, G * C), BF16),
        compiler_params=_params(("arbitrary", "arbitrary"), 14 * S * C * 4),
        name="pool_mixer",
    )(fu3, w_pool, pool_scale3)


def _route_kernel(lg_ref, dest_ref, wts_ref, meta_ref, m_all, m_top, tri, *,
                  n_experts, tile, chunk):
    N = lg_ref.shape[0]
    n_chunks = N // chunk
    lane = lax.broadcasted_iota(I32, (chunk, LANES), 1).astype(F32)
    lane1 = lax.broadcasted_iota(I32, (1, LANES), 1).astype(F32)
    r = lax.broadcasted_iota(I32, (chunk, chunk), 0)
    c = lax.broadcasted_iota(I32, (chunk, chunk), 1)
    tri[...] = jnp.where(c < r, 1.0, 0.0).astype(BF16)

    def select(ci, counts):
        rows = pl.ds(pl.multiple_of(ci * chunk, chunk), chunk)
        lg = jnp.where(lane < n_experts, lg_ref[rows, :], -jnp.inf)
        v1 = jnp.max(lg, axis=-1, keepdims=True)
        i1 = jnp.min(jnp.where(lg == v1, lane, float(LANES)), axis=-1, keepdims=True)
        first = lane == i1
        lg2 = jnp.where(first, -jnp.inf, lg)
        v2 = jnp.max(lg2, axis=-1, keepdims=True)
        i2 = jnp.min(jnp.where(lg2 == v2, lane, float(LANES)), axis=-1, keepdims=True)
        second = lane == i2
        e2 = jnp.exp(v2 - v1)
        w1 = 1.0 / (1.0 + e2)
        w2 = e2 / (1.0 + e2)
        wts_ref[rows, :] = jnp.where(lane == 0.0, w1, jnp.where(lane == 1.0, w2, 0.0))
        both = jnp.where(first | second, 1.0, 0.0)
        m_all[rows, :] = both.astype(BF16)
        m_top[rows, :] = jnp.where(first, 1.0, 0.0).astype(BF16)
        return counts + jnp.sum(both, axis=0, keepdims=True)

    counts = lax.fori_loop(0, n_chunks, select, jnp.zeros((1, LANES), F32))
    padded = jnp.ceil(counts / tile) * tile
    start = jnp.zeros((1, LANES), F32)
    tile_row = lane1 * tile
    tile_expert = jnp.zeros((1, LANES), F32)
    last_used = jnp.zeros((1, 1), F32)
    running = jnp.zeros((1, 1), F32)
    for e in range(n_experts):
        rows_e = jnp.sum(jnp.where(lane1 == e, padded, 0.0), axis=-1, keepdims=True)
        start = start + jnp.where(lane1 > e, rows_e, 0.0)
        running = running + rows_e
        tile_expert = tile_expert + jnp.where(tile_row >= running, 1.0, 0.0)
        last_used = jnp.where(rows_e > 0.0, float(e), last_used)
    tile_expert = jnp.minimum(tile_expert, last_used)
    n_tiles = running / tile
    sub = lax.broadcasted_iota(I32, (SUBLANES, LANES), 0)
    meta_ref[...] = jnp.where(sub == 0, tile_expert, jnp.where(sub == 1, n_tiles, 0.0)).astype(I32)

    def place(ci, carry):
        rows = pl.ds(pl.multiple_of(ci * chunk, chunk), chunk)
        both = m_all[rows, :]
        top = m_top[rows, :].astype(F32)
        rank = jnp.dot(tri[...], both, preferred_element_type=F32)
        pos = start + carry + rank
        both_f = both.astype(F32)
        d1 = jnp.sum(top * pos, axis=-1, keepdims=True)
        d2 = jnp.sum((both_f - top) * pos, axis=-1, keepdims=True)
        dest_ref[rows, :] = jnp.where(lane == 0.0, d1, jnp.where(lane == 1.0, d2, 0.0)).astype(I32)
        return carry + jnp.sum(both_f, axis=0, keepdims=True)

    lax.fori_loop(0, n_chunks, place, jnp.zeros((1, LANES), F32))


def _route(logits, n_experts, tile):
    N = logits.shape[0]
    chunk = min(N, 512)
    return pl.pallas_call(
        functools.partial(_route_kernel, n_experts=n_experts, tile=tile, chunk=chunk),
        out_shape=[jax.ShapeDtypeStruct((N, LANES), I32),
                   jax.ShapeDtypeStruct((N, LANES), F32),
                   jax.ShapeDtypeStruct((SUBLANES, LANES), I32)],
        scratch_shapes=[pltpu.VMEM((N, LANES), BF16), pltpu.VMEM((N, LANES), BF16),
                        pltpu.VMEM((chunk, chunk), BF16)],
        compiler_params=pltpu.CompilerParams(
            vmem_limit_bytes=int(min(max(N * LANES * 4 * 8, 16 * 2**20), VMEM_CAP_BYTES))),
        name="moe_route",
    )(logits)


def _gather_kernel(dest_ref, nt_ref, h_hbm, o_ref, src, stage, sems, *, tile, n_rows, n_assign):
    t = pl.program_id(0)
    n_used = nt_ref[0]
    slot = t % 2

    def issue_tile(tt, to_slot):
        base = tt * tile

        def issue(rr, carry):
            tok = src[base + rr]
            pltpu.make_async_copy(h_hbm.at[pl.ds(tok, 1), :],
                                  stage.at[to_slot, pl.ds(rr, 1), :], sems.at[to_slot]).start()
            return carry
        lax.fori_loop(0, tile, issue, 0, unroll=8)

    @pl.when(t == 0)
    def _invert_dest():
        def clear(rr, carry):
            src[rr] = 0
            return carry
        lax.fori_loop(0, n_rows, clear, 0, unroll=8)

        def scatter(a, carry):
            src[dest_ref[a]] = a // TOP_K
            return carry
        lax.fori_loop(0, n_assign, scatter, 0, unroll=8)
        issue_tile(0, 0)

    @pl.when(t + 1 < n_used)
    def _prefetch_next():
        issue_tile(t + 1, 1 - slot)

    @pl.when(t < n_used)
    def _gather():
        pltpu.make_async_copy(h_hbm.at[pl.ds(0, tile), :], stage.at[slot], sems.at[slot]).wait()
        o_ref[...] = stage[slot].astype(o_ref.dtype)

    @pl.when(t >= n_used)
    def _unused_tile():
        o_ref[...] = jnp.zeros(o_ref.shape, o_ref.dtype)


def _gather(dest_flat, n_tiles, h2, tile, max_tiles):
    N, D = h2.shape
    n_rows = max_tiles * tile
    grid_spec = pltpu.PrefetchScalarGridSpec(
        num_scalar_prefetch=2,
        grid=(max_tiles,),
        in_specs=[pl.BlockSpec(memory_space=pl.ANY)],
        out_specs=pl.BlockSpec((tile, D), lambda t, dest, nt: (t, 0)),
        scratch_shapes=[pltpu.SMEM((n_rows,), I32), pltpu.VMEM((2, tile, D), F32),
                        pltpu.SemaphoreType.DMA((2,))],
    )
    return pl.pallas_call(
        functools.partial(_gather_kernel, tile=tile, n_rows=n_rows, n_assign=dest_flat.shape[0]),
        grid_spec=grid_spec,
        out_shape=jax.ShapeDtypeStruct((n_rows, D), BF16),
        compiler_params=_params(("arbitrary",), 4 * tile * D * 4),
        name="moe_gather",
    )(dest_flat, n_tiles, h2)


def _grouped_kernel(te_ref, nt_ref, a_ref, *rest, n_w, layer, tn, max_tiles, compute):
    w_hbm = rest[:n_w]
    o_ref = rest[n_w]
    w_buf = rest[n_w + 1:2 * n_w + 1]
    sems, state = rest[2 * n_w + 1:]
    j = pl.program_id(0)
    t = pl.program_id(1)
    n_used = nt_ref[0]

    def fetch(e, jj, slot):
        return [pltpu.make_async_copy(w_hbm[wi].at[layer, e, :, pl.ds(jj * tn, tn)],
                                      w_buf[wi].at[slot], sems.at[wi, slot])
                for wi in range(n_w)]

    @pl.when((j == 0) & (t == 0))
    def _first_fetch():
        state[0] = 0
        for cp in fetch(te_ref[0], 0, 0):
            cp.start()

    e = te_ref[t]
    run_start = (t < n_used) & ((t == 0) | (e != te_ref[jnp.maximum(t - 1, 0)]))

    @pl.when(run_start)
    def _swap_weights():
        slot = state[0]
        for cp in fetch(e, j, slot):
            cp.wait()
        t_next = lax.while_loop(
            lambda tt: (tt < n_used) & (te_ref[jnp.minimum(tt, max_tiles - 1)] == e),
            lambda tt: tt + 1, t + 1)
        same_sweep = t_next < n_used

        @pl.when(same_sweep)
        def _next_expert():
            for cp in fetch(te_ref[jnp.minimum(t_next, max_tiles - 1)], j, 1 - slot):
                cp.start()

        @pl.when(jnp.logical_not(same_sweep) & (j + 1 < pl.num_programs(0)))
        def _next_sweep():
            for cp in fetch(te_ref[0], j + 1, 1 - slot):
                cp.start()

        state[1] = slot
        state[0] = 1 - slot

    @pl.when(t < n_used)
    def _compute():
        slot = state[1]
        compute(a_ref, *[wb.at[slot] for wb in w_buf], o_ref)

    @pl.when(t >= n_used)
    def _unused_tile():
        o_ref[...] = jnp.zeros(o_ref.shape, o_ref.dtype)


def _grouped(compute, a, ws, l, tile_expert, n_tiles, tile, tn, out_dtype, name):
    R, K = a.shape
    n_cols = ws[0].shape[-1]
    max_tiles = R // tile
    n_w = len(ws)
    grid_spec = pltpu.PrefetchScalarGridSpec(
        num_scalar_prefetch=2,
        grid=(n_cols // tn, max_tiles),
        in_specs=[pl.BlockSpec((tile, K), lambda j, t, te, nt: (jnp.minimum(t, nt[0] - 1), 0))]
        + [pl.BlockSpec(memory_space=pl.ANY)] * n_w,
        out_specs=pl.BlockSpec((tile, tn), lambda j, t, te, nt: (t, j)),
        scratch_shapes=[pltpu.VMEM((2, K, tn), ws[0].dtype) for _ in range(n_w)]
        + [pltpu.SemaphoreType.DMA((n_w, 2)), pltpu.SMEM((2,), I32)],
    )
    return pl.pallas_call(
        functools.partial(_grouped_kernel, n_w=n_w, layer=l, tn=tn, max_tiles=max_tiles,
                          compute=compute),
        grid_spec=grid_spec,
        out_shape=jax.ShapeDtypeStruct((R, n_cols), out_dtype),
        compiler_params=_params(
            ("arbitrary", "arbitrary"),
            _dense_vmem(tile, K, tn, n_w, ws[0].dtype.itemsize,
                        jnp.dtype(out_dtype).itemsize, 2)),
        name=name,
    )(tile_expert, n_tiles, a, *ws)


def _combine_kernel(dest_ref, ys_hbm, wts_ref, x_ref, g_ref, gate_ref, o_ref,
                    stage, sems, *, tile, tiles_per_seq, n_steps):
    t = pl.program_id(0) * tiles_per_seq + pl.program_id(1)
    slot = t % 2

    def issue_tile(tt, to_slot):
        n0 = tt * tile

        def issue(rr, carry):
            a = (n0 + rr) * TOP_K
            for k in range(TOP_K):
                pltpu.make_async_copy(ys_hbm.at[pl.ds(dest_ref[a + k], 1), :],
                                      stage.at[to_slot, k, pl.ds(rr, 1), :],
                                      sems.at[to_slot, k]).start()
            return carry
        lax.fori_loop(0, tile, issue, 0, unroll=4)

    @pl.when(t == 0)
    def _first_tile():
        issue_tile(0, 0)

    @pl.when(t + 1 < n_steps)
    def _prefetch_next():
        issue_tile(t + 1, 1 - slot)

    for k in range(TOP_K):
        pltpu.make_async_copy(ys_hbm.at[pl.ds(0, tile), :], stage.at[slot, k],
                              sems.at[slot, k]).wait()
    w = wts_ref[...]
    y = w[:, 0:1] * stage[slot, 0] + w[:, 1:2] * stage[slot, 1]
    o_ref[...] = x_ref[...] + gate_ref[...] * _rms(y, g_ref[...])


def _combine(dest_flat, ys, wts3, x, gains, mod, l, g_idx, gate_idx):
    B, S, D = x.shape
    tile = min(S, 256)
    row = pl.BlockSpec((None, tile, D), lambda b, i, dest: (b, i, 0))
    grid_spec = pltpu.PrefetchScalarGridSpec(
        num_scalar_prefetch=1,
        grid=(B, S // tile),
        in_specs=[pl.BlockSpec(memory_space=pl.ANY),
                  pl.BlockSpec((None, tile, LANES), lambda b, i, dest: (b, i, 0)),
                  row,
                  pl.BlockSpec((None, None, 1, D), lambda b, i, dest: (l, g_idx, 0, 0)),
                  pl.BlockSpec((None, None, None, 1, D),
                               lambda b, i, dest: (l, b, gate_idx, 0, 0))],
        out_specs=row,
        scratch_shapes=[pltpu.VMEM((2, TOP_K, tile, D), F32),
                        pltpu.SemaphoreType.DMA((2, TOP_K))],
    )
    return pl.pallas_call(
        functools.partial(_combine_kernel, tile=tile, tiles_per_seq=S // tile,
                          n_steps=B * (S // tile)),
        grid_spec=grid_spec,
        out_shape=jax.ShapeDtypeStruct((B, S, D), F32),
        compiler_params=_params(("arbitrary", "arbitrary"), 12 * tile * D * 4),
        name="moe_combine",
    )(dest_flat, ys, wts3, x, gains, mod)


def kernel(x, c, norm_gains, w_mod, b_mod, w_in, b_forget, w_pool, pool_scale, w_out,
           w_ffn_gate, w_ffn_up, w_ffn_down, w_router, w_exp_gate, w_exp_up, w_exp_down):
    B, S, D = x.shape
    L = w_mod.shape[0]
    N = B * S
    H = b_forget.shape[1]
    A = H * HEAD_DIM
    P = pool_scale.shape[1]
    assert A + P == D and w_in.shape[2] == 3 * A + H + P and H <= LANES
    assert P % LANES == 0 and A % LANES == 0
    E = w_router.shape[2]
    tm = min(N, 1024)
    tm_wide = min(N, 2048)

    mod = _mod(c, w_mod, b_mod)
    gains = norm_gains.reshape(L, 4, 1, D)
    h = _norm_mod(x, gains, mod, 0, 0, 0, 1, BF16)

    qk_scale = jnp.concatenate([jnp.full((1, A), HEAD_DIM ** -0.5 * LOG2_E, F32),
                                jnp.ones((1, 2 * A), F32)], axis=1)
    pool_scale3 = pool_scale.reshape(L, 1, P)
    w_in_t = jnp.swapaxes(w_in, 1, 2)

    for l in range(L):
        h2 = h.reshape(N, D)
        qkv_h = _qkv_proj(h2, w_in_t, l, 3 * A, tm_wide, min(3 * A, 512), qk_scale)
        w_fu_t = jnp.concatenate([w_in_t[l, 3 * A + H:], w_in_t[l, 3 * A:3 * A + H],
                                  jnp.zeros((LANES - H, D), F32)], axis=0)
        tn_fu = LANES * 3 if (P + LANES) % (LANES * 3) == 0 else LANES
        fu = _matmul_nt(h2, w_fu_t, tm, tn_fu, F32, "in_proj_gate_pool")
        fu3 = fu.reshape(B, S, P + LANES)
        b_pad = jnp.pad(b_forget[l], (0, LANES - H)).reshape(1, LANES)
        f_sh, f_hs = _fox_prep(fu3, b_pad, P // LANES)
        attn = _attention(qkv_h.reshape(3 * H, B, S, HEAD_DIM), f_sh,
                          f_hs.reshape(B, LANES, 1, S), H)
        pooled = _pool(fu3, w_pool, pool_scale3, l)
        mixed = _matmul_two(attn.reshape(N, A), pooled.reshape(N, P), w_out, (l,), tm_wide,
                            min(D, 512), F32, "out_proj")

        i = l // 2
        if l % 2 == 0:
            x, h = _resid_norm(x, mixed.reshape(B, S, D), gains, mod, l, 1, 2,
                               nxt=(l, 2, 3, 4))
            FF = w_ffn_gate.shape[2]
            tn_ff = 256 if FF % 256 == 0 else LANES
            act = _gateup(h.reshape(N, D), w_ffn_gate, w_ffn_up, (i,), tm_wide, tn_ff)
            y = _down(act, w_ffn_down, (i,), tm, min(D, 256))
            y3 = y.reshape(B, S, D)
            if l + 1 < L:
                x, h = _resid_norm(x, y3, gains, mod, l, 3, 5, nxt=(l + 1, 0, 0, 1))
            else:
                (x,) = _resid_norm(x, y3, gains, mod, l, 3, 5)
        else:
            wr_pad = jnp.pad(w_router[i], ((0, 0), (0, LANES - E)))
            x, hf, logits = _resid_norm(x, mixed.reshape(B, S, D), gains, mod, l, 1, 2,
                                        nxt=(l, 2, 3, 4), h_dtype=F32, w_router=wr_pad)
            tile = min(N, 512)
            max_tiles = (N * TOP_K) // tile + E
            assert max_tiles <= LANES
            dest, wts, meta = _route(logits.reshape(N, LANES), E, tile)
            dest_flat = dest[:, :TOP_K].reshape(N * TOP_K)
            tile_expert = meta[0, :max_tiles]
            n_tiles = meta[1, :1]
            xs = _gather(dest_flat, n_tiles, hf.reshape(N, D), tile, max_tiles)
            tn_e = min(w_exp_gate.shape[3], 512)
            act = _grouped(_gateup_kernel, xs, (w_exp_gate, w_exp_up), i,
                           tile_expert, n_tiles, tile, tn_e, BF16, "moe_gate_up")
            ys = _grouped(_mm_kernel, act, (w_exp_down,), i,
                          tile_expert, n_tiles, tile, min(D, 1024), F32, "moe_down")
            x = _combine(dest_flat, ys, wts.reshape(B, S, LANES), x, gains, mod, l, 3, 5)
            if l + 1 < L:
                h = _norm_mod(x, gains, mod, l + 1, 0, 0, 1, BF16)
    return x
```

```python
import functools

import jax
import jax.numpy as jnp
from jax import lax
from jax.experimental import pallas as pl
from jax.experimental.pallas import tpu as pltpu

F32 = jnp.float32
BF16 = jnp.bfloat16
I32 = jnp.int32

HEAD_DIM = 128
POOL_WINDOWS = (2, 4, 8, 16)
N_MOD = 6
TOP_K = 2
RMS_EPS = 1e-6
LOG2_E = 1.4426950408889634
MOE_ROW_STEP = 128

LANES = 128
SUBLANES = 8
V7X_VMEM_BYTES = 64 * 2**20
VMEM_CAP_BYTES = V7X_VMEM_BYTES - 6 * 2**20


def _params(semantics, vmem_bytes):
    limit = int(min(max(vmem_bytes * 5 // 4, 16 * 2**20), VMEM_CAP_BYTES))
    return pltpu.CompilerParams(dimension_semantics=semantics, vmem_limit_bytes=limit)


def _rms(x, gain):
    inv = lax.rsqrt(jnp.mean(x * x, axis=-1, keepdims=True) + RMS_EPS)
    return x * inv * gain


def _mod_kernel(c_ref, w_ref, b_ref, o_ref):
    c = c_ref[...]
    c_act = (c * jax.nn.sigmoid(c)).astype(BF16)
    o_ref[...] = jnp.dot(c_act, w_ref[...].astype(BF16),
                         preferred_element_type=F32) + b_ref[...]


def _mod(c, w_mod, b_mod):
    B, D = c.shape
    L, _, NM = w_mod.shape
    rows = -(-B // SUBLANES) * SUBLANES
    c_pad = jnp.pad(c, ((0, rows - B), (0, 0)))
    tn = 512
    out = pl.pallas_call(
        _mod_kernel,
        grid=(L, NM // tn),
        in_specs=[
            pl.BlockSpec((rows, D), lambda l, j: (0, 0)),
            pl.BlockSpec((None, D, tn), lambda l, j: (l, 0, j)),
            pl.BlockSpec((None, 1, tn), lambda l, j: (l, 0, j)),
        ],
        out_specs=pl.BlockSpec((None, rows, tn), lambda l, j: (l, 0, j)),
        out_shape=jax.ShapeDtypeStruct((L, rows, NM), F32),
        compiler_params=_params(("arbitrary", "arbitrary"), 3 * D * tn * 4),
        name="adaln_mod",
    )(c_pad, w_mod, b_mod.reshape(L, 1, NM))
    return out[:, :B].reshape(L, B, N_MOD, 1, D)


def _mod_spec(l, idx, D):
    return pl.BlockSpec((None, None, None, 1, D), lambda b, i: (l, b, idx, 0, 0))


def _gain_spec(l, idx, D):
    return pl.BlockSpec((None, None, 1, D), lambda b, i: (l, idx, 0, 0))


def _norm_mod_kernel(x_ref, g_ref, sc_ref, sh_ref, o_ref):
    h = _rms(x_ref[...], g_ref[...])
    o_ref[...] = (h * (1.0 + sc_ref[...]) + sh_ref[...]).astype(o_ref.dtype)


def _norm_mod(x, gains, mod, l, gain_idx, shift_idx, scale_idx, out_dtype):
    B, S, D = x.shape
    ts = min(S, 256)
    row = pl.BlockSpec((None, ts, D), lambda b, i: (b, i, 0))
    return pl.pallas_call(
        _norm_mod_kernel,
        grid=(B, S // ts),
        in_specs=[row, _gain_spec(l, gain_idx, D), _mod_spec(l, scale_idx, D),
                  _mod_spec(l, shift_idx, D)],
        out_specs=row,
        out_shape=jax.ShapeDtypeStruct((B, S, D), out_dtype),
        compiler_params=_params(("arbitrary", "arbitrary"), 4 * ts * D * 4),
        name="pre_norm",
    )(x, gains, mod, mod)


def _split_bf16(v):
    hi = v.astype(BF16)
    lo = (v - hi.astype(F32)).astype(BF16)
    return hi, lo


def _resid_norm_kernel(*refs, emit_h, emit_logits):
    x_ref, y_ref, ga_ref, gate_ref = refs[:4]
    pos = 4
    if emit_h:
        gb_ref, sc_ref, sh_ref = refs[pos:pos + 3]
        pos += 3
    if emit_logits:
        wr_ref = refs[pos]
        pos += 1
    xo_ref = refs[pos]
    pos += 1
    x_new = x_ref[...] + gate_ref[...] * _rms(y_ref[...], ga_ref[...])
    xo_ref[...] = x_new
    if emit_h:
        h_ref = refs[pos]
        pos += 1
        h = _rms(x_new, gb_ref[...]) * (1.0 + sc_ref[...]) + sh_ref[...]
        h_ref[...] = h.astype(h_ref.dtype)
        if emit_logits:
            lg_ref = refs[pos]
            h_hi, h_lo = _split_bf16(h)
            w_hi, w_lo = _split_bf16(wr_ref[...])
            lg_ref[...] = (jnp.dot(h_hi, w_hi, preferred_element_type=F32)
                           + jnp.dot(h_lo, w_hi, preferred_element_type=F32)
                           + jnp.dot(h_hi, w_lo, preferred_element_type=F32))


def _resid_norm(x, y, gains, mod, la, ga_idx, gate_idx, nxt=None, h_dtype=BF16,
                w_router=None):
    B, S, D = x.shape
    ts = min(S, 256)
    row = pl.BlockSpec((None, ts, D), lambda b, i: (b, i, 0))
    in_specs = [row, row, _gain_spec(la, ga_idx, D), _mod_spec(la, gate_idx, D)]
    args = [x, y, gains, mod]
    out_specs = [row]
    out_shape = [jax.ShapeDtypeStruct((B, S, D), F32)]
    emit_h = nxt is not None
    emit_logits = w_router is not None
    if emit_h:
        lb, gb_idx, shift_idx, scale_idx = nxt
        in_specs += [_gain_spec(lb, gb_idx, D), _mod_spec(lb, scale_idx, D),
                     _mod_spec(lb, shift_idx, D)]
        args += [gains, mod, mod]
        out_specs.append(row)
        out_shape.append(jax.ShapeDtypeStruct((B, S, D), h_dtype))
    if emit_logits:
        in_specs.append(pl.BlockSpec((D, LANES), lambda b, i: (0, 0)))
        args.append(w_router)
        out_specs.append(pl.BlockSpec((None, ts, LANES), lambda b, i: (b, i, 0)))
        out_shape.append(jax.ShapeDtypeStruct((B, S, LANES), F32))
    outs = pl.pallas_call(
        functools.partial(_resid_norm_kernel, emit_h=emit_h, emit_logits=emit_logits),
        grid=(B, S // ts),
        in_specs=in_specs,
        out_specs=out_specs,
        out_shape=out_shape,
        compiler_params=_params(("arbitrary", "arbitrary"), 10 * ts * D * 4),
        name="resid_norm",
    )(*args)
    return outs


def _mm_kernel(a_ref, w_ref, o_ref):
    o_ref[...] = jnp.dot(a_ref[...], w_ref[...].astype(BF16),
                         preferred_element_type=F32).astype(o_ref.dtype)


def _nt_dot(a, wt):
    return lax.dot_general(a, wt, (((1,), (1,)), ((), ())), preferred_element_type=F32)


def _mm_nt_kernel(a_ref, wt_ref, o_ref):
    o_ref[...] = _nt_dot(a_ref[...], wt_ref[...].astype(BF16)).astype(o_ref.dtype)


def _qkv_kernel(a_ref, wt_ref, cs_ref, o_ref):
    acc = _nt_dot(a_ref[...], wt_ref[...].astype(BF16)) * cs_ref[...]
    for hh in range(o_ref.shape[0]):
        o_ref[hh] = acc[:, hh * HEAD_DIM:(hh + 1) * HEAD_DIM].astype(o_ref.dtype)


def _mm_two_kernel(a1_ref, a2_ref, w_ref, o_ref):
    k1 = a1_ref.shape[1]
    w = w_ref[...].astype(BF16)
    o_ref[...] = (jnp.dot(a1_ref[...], w[:k1], preferred_element_type=F32)
                  + jnp.dot(a2_ref[...], w[k1:], preferred_element_type=F32)).astype(o_ref.dtype)


def _gateup_kernel(a_ref, wg_ref, wu_ref, o_ref):
    a = a_ref[...]
    g = jnp.dot(a, wg_ref[...].astype(BF16), preferred_element_type=F32)
    u = jnp.dot(a, wu_ref[...].astype(BF16), preferred_element_type=F32)
    o_ref[...] = (g * jax.nn.sigmoid(g) * u).astype(o_ref.dtype)


def _w_spec(w, tn, lead, col_block_offset=0):
    K = w.shape[-2]
    nlead = len(lead)
    return pl.BlockSpec((None,) * nlead + (K, tn),
                        lambda i, j: tuple(lead) + (0, j + col_block_offset))


def _a_spec(tm, K):
    return pl.BlockSpec((tm, K), lambda i, j: (i, 0), pipeline_mode=pl.Buffered(1))


def _dense_vmem(tm, K, tn, n_w, w_bytes, out_bytes, a_buffers):
    a = a_buffers * tm * K * 2
    w = n_w * (2 * K * tn * w_bytes + K * tn * 2)
    o = 2 * tm * tn * out_bytes + n_w * tm * tn * 4
    return a + w + o


def _qkv_proj(a, wt, l, n_cols, tm, tn, colscale):
    M, K = a.shape
    hpb = tn // HEAD_DIM
    return pl.pallas_call(
        _qkv_kernel,
        grid=(M // tm, n_cols // tn),
        in_specs=[_a_spec(tm, K),
                  pl.BlockSpec((None, tn, K), lambda i, j: (l, j, 0)),
                  pl.BlockSpec((1, tn), lambda i, j: (0, j))],
        out_specs=pl.BlockSpec((hpb, tm, HEAD_DIM), lambda i, j: (j, i, 0)),
        out_shape=jax.ShapeDtypeStruct((n_cols // HEAD_DIM, M, HEAD_DIM), BF16),
        compiler_params=_params(("arbitrary", "arbitrary"),
                                _dense_vmem(tm, K, tn, 1, wt.dtype.itemsize, 2, 1)),
        name="in_proj_qkv",
    )(a, wt, colscale)


def _matmul_nt(a, wt, tm, tn, out_dtype, name):
    M, K = a.shape
    n_cols = wt.shape[0]
    return pl.pallas_call(
        _mm_nt_kernel,
        grid=(M // tm, n_cols // tn),
        in_specs=[pl.BlockSpec((tm, K), lambda i, j: (i, 0)),
                  pl.BlockSpec((tn, K), lambda i, j: (j, 0))],
        out_specs=pl.BlockSpec((tm, tn), lambda i, j: (i, j)),
        out_shape=jax.ShapeDtypeStruct((M, n_cols), out_dtype),
        compiler_params=_params(
            ("arbitrary", "arbitrary"),
            _dense_vmem(tm, K, tn, 1, wt.dtype.itemsize, jnp.dtype(out_dtype).itemsize, 2)),
        name=name,
    )(a, wt)


def _matmul_two(a1, a2, w, lead, tm, tn, out_dtype, name):
    M, K1 = a1.shape
    K2 = a2.shape[1]
    n_cols = w.shape[-1]
    return pl.pallas_call(
        _mm_two_kernel,
        grid=(M // tm, n_cols // tn),
        in_specs=[_a_spec(tm, K1), _a_spec(tm, K2), _w_spec(w, tn, lead)],
        out_specs=pl.BlockSpec((tm, tn), lambda i, j: (i, j)),
        out_shape=jax.ShapeDtypeStruct((M, n_cols), out_dtype),
        compiler_params=_params(
            ("arbitrary", "arbitrary"),
            _dense_vmem(tm, K1 + K2, tn, 1, w.dtype.itemsize, jnp.dtype(out_dtype).itemsize, 1)),
        name=name,
    )(a1, a2, w)


def _gateup(a, wg, wu, lead, tm, tn, name="ffn_gate_up"):
    M, K = a.shape
    n_cols = wg.shape[-1]
    return pl.pallas_call(
        _gateup_kernel,
        grid=(M // tm, n_cols // tn),
        in_specs=[_a_spec(tm, K), _w_spec(wg, tn, lead), _w_spec(wu, tn, lead)],
        out_specs=pl.BlockSpec((tm, tn), lambda i, j: (i, j)),
        out_shape=jax.ShapeDtypeStruct((M, n_cols), BF16),
        compiler_params=_params(("arbitrary", "arbitrary"),
                                _dense_vmem(tm, K, tn, 2, wg.dtype.itemsize, 2, 1)),
        name=name,
    )(a, wg, wu)


def _down(a, w, lead, tm, tn, name="ffn_down"):
    M, K = a.shape
    n_cols = w.shape[-1]
    return pl.pallas_call(
        _mm_kernel,
        grid=(M // tm, n_cols // tn),
        in_specs=[_a_spec(tm, K), _w_spec(w, tn, lead)],
        out_specs=pl.BlockSpec((tm, tn), lambda i, j: (i, j)),
        out_shape=jax.ShapeDtypeStruct((M, n_cols), F32),
        compiler_params=_params(("arbitrary", "arbitrary"),
                                _dense_vmem(tm, K, tn, 1, w.dtype.itemsize, 4, 1)),
        name=name,
    )(a, w)


def _fox_prep_kernel(f_ref, b_ref, fsh_ref, fhs_ref):
    z = f_ref[...] + b_ref[...]
    acc = jnp.minimum(z, 0.0) - jnp.log1p(jnp.exp(-jnp.abs(z)))
    S = acc.shape[0]
    row = lax.broadcasted_iota(I32, acc.shape, 0)
    d = 1
    while d < S:
        acc = acc + jnp.where(row >= d, pltpu.roll(acc, d, 0), 0.0)
        d *= 2
    acc = acc * LOG2_E
    fsh_ref[...] = acc
    fhs_ref[...] = acc.T


def _fox_prep(fu3, b_pad, col_block):
    B, S, _ = fu3.shape
    return pl.pallas_call(
        _fox_prep_kernel,
        grid=(B,),
        in_specs=[pl.BlockSpec((None, S, LANES), lambda b: (b, 0, col_block)),
                  pl.BlockSpec((1, LANES), lambda b: (0, 0))],
        out_specs=[pl.BlockSpec((None, S, LANES), lambda b: (b, 0, 0)),
                   pl.BlockSpec((None, LANES, S), lambda b: (b, 0, 0))],
        out_shape=[jax.ShapeDtypeStruct((B, S, LANES), F32),
                   jax.ShapeDtypeStruct((B, LANES, S), F32)],
        compiler_params=_params(("arbitrary",), 12 * S * LANES * 4),
        name="fox_prep",
    )(fu3, b_pad)


def _attn_kernel(q_ref, k_ref, v_ref, fq_ref, fk_ref, o_ref, fq_s, *, blk):
    h = pl.program_id(1)
    S = q_ref.shape[0]
    n_blk = S // blk
    lane = lax.broadcasted_iota(I32, fq_ref.shape, 1)
    fq_col = jnp.sum(jnp.where(lane == h, fq_ref[...], 0.0), axis=-1, keepdims=True)
    fq_s[...] = jnp.broadcast_to(fq_col, fq_s.shape)
    def logits(row_lo, n_rows, key_lo, n_keys):
        q = q_ref[row_lo:row_lo + n_rows, :]
        fq = fq_s[row_lo:row_lo + n_rows, :]
        s = lax.dot_general(q, k_ref[key_lo:key_lo + n_keys, :], (((1,), (1,)), ((), ())),
                            preferred_element_type=F32)
        chunks = []
        for c in range(n_keys // LANES):
            lo = key_lo + c * LANES
            sc = s[:, c * LANES:(c + 1) * LANES] + (fq - fk_ref[:, lo:lo + LANES])
            if lo + LANES - 1 > row_lo:
                r = lax.broadcasted_iota(I32, (n_rows, LANES), 0) + row_lo
                k = lax.broadcasted_iota(I32, (n_rows, LANES), 1) + lo
                sc = jnp.where(k <= r, sc, -jnp.inf)
            chunks.append(sc)
        return chunks

    def online_update(state, chunks, v):
        m_row = jnp.max(functools.reduce(jnp.maximum, chunks), axis=-1, keepdims=True)
        m_new = m_row if state is None else jnp.maximum(state[0], m_row)
        m_new = jnp.broadcast_to(m_new, chunks[0].shape)
        ps = [jnp.exp2(sc - m_new) for sc in chunks]
        p_sum = functools.reduce(lambda a, b: a + b, ps)
        pv = jnp.dot(jnp.concatenate(ps, axis=1).astype(BF16), v, preferred_element_type=F32)
        if state is None:
            return m_new, p_sum, pv
        alpha = jnp.exp2(state[0] - m_new)
        return m_new, alpha * state[1] + p_sum, alpha * state[2] + pv

    for i in range(n_blk):
        r0 = i * blk
        state = None
        for j in range(i):
            k0 = j * blk
            state = online_update(state, logits(r0, blk, k0, blk), v_ref[k0:k0 + blk, :])
        _, l_b, acc = online_update(state, logits(r0, blk, r0, blk), v_ref[r0:r0 + blk, :])
        l_row = jnp.sum(l_b, axis=-1, keepdims=True)
        o_ref[r0:r0 + blk, :] = (acc / l_row).astype(o_ref.dtype)


def _attention(qkv_h, f_sh, f_hs4, n_heads):
    _, B, S, _ = qkv_h.shape
    H = n_heads
    blk = min(S, 512)
    head = lambda off: pl.BlockSpec((None, None, S, HEAD_DIM), lambda b, h: (off + h, b, 0, 0))
    return pl.pallas_call(
        functools.partial(_attn_kernel, blk=blk),
        grid=(B, H),
        in_specs=[head(0), head(H), head(2 * H),
                  pl.BlockSpec((None, S, LANES), lambda b, h: (b, 0, 0)),
                  pl.BlockSpec((None, None, 1, S), lambda b, h: (b, h, 0, 0))],
        out_specs=pl.BlockSpec((None, S, HEAD_DIM), lambda b, h: (b, 0, h)),
        out_shape=jax.ShapeDtypeStruct((B, S, H * HEAD_DIM), BF16),
        scratch_shapes=[pltpu.VMEM((S, LANES), F32)],
        compiler_params=_params(("arbitrary", "arbitrary"),
                                8 * S * HEAD_DIM * 2 + 3 * S * LANES * 4 + 8 * blk * blk * 4),
        name="fox_attention",
    )(qkv_h, qkv_h, qkv_h, f_sh, f_hs4)


def _pool_kernel(u_ref, w_ref, sc_ref, o_ref):
    g = pl.program_id(1)
    u = u_ref[...]
    row = lax.broadcasted_iota(I32, u.shape, 0)

    def shifted(v, d):
        return jnp.where(row >= d, pltpu.roll(v, d, 0), 0.0)

    s2 = u + shifted(u, 1)
    s4 = s2 + shifted(s2, 2)
    s8 = s4 + shifted(s4, 4)
    s16 = s8 + shifted(s8, 8)
    win = jnp.where(g == 0, s2, jnp.where(g == 1, s4, jnp.where(g == 2, s8, s16)))
    width = lax.shift_left(jnp.int32(POOL_WINDOWS[0]), g)
    count = jnp.minimum(row + 1, width).astype(F32)
    pooled = win / count - u
    y = jnp.dot(pooled.astype(BF16), w_ref[...].astype(BF16), preferred_element_type=F32)
    o_ref[...] = (y * sc_ref[...]).astype(o_ref.dtype)


def _pool(fu3, w_pool, pool_scale3, l):
    B, S, _ = fu3.shape
    G, C = w_pool.shape[1], w_pool.shape[2]
    assert G == len(POOL_WINDOWS) and all(
        w == POOL_WINDOWS[0] << k for k, w in enumerate(POOL_WINDOWS))
    return pl.pallas_call(
        _pool_kernel,
        grid=(B, G),
        in_specs=[pl.BlockSpec((None, S, C), lambda b, g: (b, 0, g)),
                  pl.BlockSpec((None, None, C, C), lambda b, g: (l, g, 0, 0)),
                  pl.BlockSpec((None, 1, C), lambda b, g: (l, 0, g))],
        out_specs=pl.BlockSpec((None, S, C), lambda b, g: (b, 0, g)),
        out_shape=jax.ShapeDtypeStruct((B, S, G * C), BF16),
        compiler_params=_params(("arbitrary", "arbitrary"), 14 * S * C * 4),
        name="pool_mixer",
    )(fu3, w_pool, pool_scale3)


def _route_kernel(lg_ref, dest_ref, wts_ref, meta_ref, m_all, m_top, tri, *,
                  n_experts, tile, chunk):
    N = lg_ref.shape[0]
    n_chunks = N // chunk
    lane = lax.broadcasted_iota(I32, (chunk, LANES), 1).astype(F32)
    lane1 = lax.broadcasted_iota(I32, (1, LANES), 1).astype(F32)
    r = lax.broadcasted_iota(I32, (chunk, chunk), 0)
    c = lax.broadcasted_iota(I32, (chunk, chunk), 1)
    tri[...] = jnp.where(c < r, 1.0, 0.0).astype(BF16)

    def select(ci, counts):
        rows = pl.ds(pl.multiple_of(ci * chunk, chunk), chunk)
        lg = jnp.where(lane < n_experts, lg_ref[rows, :], -jnp.inf)
        v1 = jnp.max(lg, axis=-1, keepdims=True)
        i1 = jnp.min(jnp.where(lg == v1, lane, float(LANES)), axis=-1, keepdims=True)
        first = lane == i1
        lg2 = jnp.where(first, -jnp.inf, lg)
        v2 = jnp.max(lg2, axis=-1, keepdims=True)
        i2 = jnp.min(jnp.where(lg2 == v2, lane, float(LANES)), axis=-1, keepdims=True)
        second = lane == i2
        e2 = jnp.exp(v2 - v1)
        w1 = 1.0 / (1.0 + e2)
        w2 = e2 / (1.0 + e2)
        wts_ref[rows, :] = jnp.where(lane == 0.0, w1, jnp.where(lane == 1.0, w2, 0.0))
        both = jnp.where(first | second, 1.0, 0.0)
        m_all[rows, :] = both.astype(BF16)
        m_top[rows, :] = jnp.where(first, 1.0, 0.0).astype(BF16)
        return counts + jnp.sum(both, axis=0, keepdims=True)

    counts = lax.fori_loop(0, n_chunks, select, jnp.zeros((1, LANES), F32))
    padded = jnp.ceil(counts / tile) * tile
    start = jnp.zeros((1, LANES), F32)
    tile_row = lane1 * tile
    tile_expert = jnp.zeros((1, LANES), F32)
    last_used = jnp.zeros((1, 1), F32)
    running = jnp.zeros((1, 1), F32)
    tile_valid = jnp.zeros((1, LANES), F32)
    for e in range(n_experts):
        rows_e = jnp.sum(jnp.where(lane1 == e, padded, 0.0), axis=-1, keepdims=True)
        count_e = jnp.sum(jnp.where(lane1 == e, counts, 0.0), axis=-1, keepdims=True)
        in_group = (tile_row >= running) & (tile_row < running + rows_e)
        tile_valid = jnp.where(in_group, jnp.minimum(running + count_e - tile_row, float(tile)),
                               tile_valid)
        start = start + jnp.where(lane1 > e, rows_e, 0.0)
        running = running + rows_e
        tile_expert = tile_expert + jnp.where(tile_row >= running, 1.0, 0.0)
        last_used = jnp.where(rows_e > 0.0, float(e), last_used)
    tile_expert = jnp.minimum(tile_expert, last_used)
    n_tiles = running / tile
    sub = lax.broadcasted_iota(I32, (SUBLANES, LANES), 0)
    meta_ref[...] = jnp.where(sub == 0, tile_expert,
                              jnp.where(sub == 1, n_tiles,
                                        jnp.where(sub == 2, tile_valid, 0.0))).astype(I32)

    def place(ci, carry):
        rows = pl.ds(pl.multiple_of(ci * chunk, chunk), chunk)
        both = m_all[rows, :]
        top = m_top[rows, :].astype(F32)
        rank = jnp.dot(tri[...], both, preferred_element_type=F32)
        pos = start + carry + rank
        both_f = both.astype(F32)
        d1 = jnp.sum(top * pos, axis=-1, keepdims=True)
        d2 = jnp.sum((both_f - top) * pos, axis=-1, keepdims=True)
        dest_ref[rows, :] = jnp.where(lane == 0.0, d1, jnp.where(lane == 1.0, d2, 0.0)).astype(I32)
        return carry + jnp.sum(both_f, axis=0, keepdims=True)

    lax.fori_loop(0, n_chunks, place, jnp.zeros((1, LANES), F32))


def _route(logits, n_experts, tile):
    N = logits.shape[0]
    chunk = min(N, 512)
    return pl.pallas_call(
        functools.partial(_route_kernel, n_experts=n_experts, tile=tile, chunk=chunk),
        out_shape=[jax.ShapeDtypeStruct((N, LANES), I32),
                   jax.ShapeDtypeStruct((N, LANES), F32),
                   jax.ShapeDtypeStruct((SUBLANES, LANES), I32)],
        scratch_shapes=[pltpu.VMEM((N, LANES), BF16), pltpu.VMEM((N, LANES), BF16),
                        pltpu.VMEM((chunk, chunk), BF16)],
        compiler_params=pltpu.CompilerParams(
            vmem_limit_bytes=int(min(max(N * LANES * 4 * 8, 16 * 2**20), VMEM_CAP_BYTES))),
        name="moe_route",
    )(logits)


def _gather_kernel(dest_ref, nt_ref, h_hbm, o_ref, src, stage, sems, *, tile, n_rows, n_assign):
    t = pl.program_id(0)
    n_used = nt_ref[0]
    slot = t % 2

    def issue_tile(tt, to_slot):
        base = tt * tile

        def issue(rr, carry):
            tok = src[base + rr]
            pltpu.make_async_copy(h_hbm.at[pl.ds(tok, 1), :],
                                  stage.at[to_slot, pl.ds(rr, 1), :], sems.at[to_slot]).start()
            return carry
        lax.fori_loop(0, tile, issue, 0, unroll=8)

    @pl.when(t == 0)
    def _invert_dest():
        def clear(rr, carry):
            src[rr] = 0
            return carry
        lax.fori_loop(0, n_rows, clear, 0, unroll=8)

        def scatter(n, carry):
            for k in range(TOP_K):
                src[dest_ref[n * TOP_K + k]] = n
            return carry
        lax.fori_loop(0, n_assign // TOP_K, scatter, 0, unroll=4)
        issue_tile(0, 0)

    @pl.when(t + 1 < n_used)
    def _prefetch_next():
        issue_tile(t + 1, 1 - slot)

    @pl.when(t < n_used)
    def _gather():
        pltpu.make_async_copy(h_hbm.at[pl.ds(0, tile), :], stage.at[slot], sems.at[slot]).wait()
        o_ref[...] = stage[slot].astype(o_ref.dtype)

    @pl.when(t >= n_used)
    def _unused_tile():
        o_ref[...] = jnp.zeros(o_ref.shape, o_ref.dtype)


def _gather(dest_flat, n_tiles, h2, tile, max_tiles):
    N, D = h2.shape
    n_rows = max_tiles * tile
    grid_spec = pltpu.PrefetchScalarGridSpec(
        num_scalar_prefetch=2,
        grid=(max_tiles,),
        in_specs=[pl.BlockSpec(memory_space=pl.ANY)],
        out_specs=pl.BlockSpec((tile, D), lambda t, dest, nt: (t, 0)),
        scratch_shapes=[pltpu.SMEM((n_rows,), I32), pltpu.VMEM((2, tile, D), F32),
                        pltpu.SemaphoreType.DMA((2,))],
    )
    return pl.pallas_call(
        functools.partial(_gather_kernel, tile=tile, n_rows=n_rows, n_assign=dest_flat.shape[0]),
        grid_spec=grid_spec,
        out_shape=jax.ShapeDtypeStruct((n_rows, D), BF16),
        compiler_params=_params(("arbitrary",), 4 * tile * D * 4),
        name="moe_gather",
    )(dest_flat, n_tiles, h2)


def _grouped_kernel(te_ref, nt_ref, nv_ref, a_ref, *rest, n_w, layer, tn, max_tiles, compute):
    w_hbm = rest[:n_w]
    o_ref = rest[n_w]
    w_buf = rest[n_w + 1:2 * n_w + 1]
    sems, state = rest[2 * n_w + 1:]
    j = pl.program_id(0)
    t = pl.program_id(1)
    n_used = nt_ref[0]

    def fetch(e, jj, slot):
        return [pltpu.make_async_copy(w_hbm[wi].at[layer, e, :, pl.ds(jj * tn, tn)],
                                      w_buf[wi].at[slot], sems.at[wi, slot])
                for wi in range(n_w)]

    @pl.when((j == 0) & (t == 0))
    def _first_fetch():
        state[0] = 0
        for cp in fetch(te_ref[0], 0, 0):
            cp.start()

    e = te_ref[t]
    run_start = (t < n_used) & ((t == 0) | (e != te_ref[jnp.maximum(t - 1, 0)]))

    @pl.when(run_start)
    def _swap_weights():
        slot = state[0]
        for cp in fetch(e, j, slot):
            cp.wait()
        t_next = lax.while_loop(
            lambda tt: (tt < n_used) & (te_ref[jnp.minimum(tt, max_tiles - 1)] == e),
            lambda tt: tt + 1, t + 1)
        same_sweep = t_next < n_used

        @pl.when(same_sweep)
        def _next_expert():
            for cp in fetch(te_ref[jnp.minimum(t_next, max_tiles - 1)], j, 1 - slot):
                cp.start()

        @pl.when(jnp.logical_not(same_sweep) & (j + 1 < pl.num_programs(0)))
        def _next_sweep():
            for cp in fetch(te_ref[0], j + 1, 1 - slot):
                cp.start()

        state[1] = slot
        state[0] = 1 - slot

    @pl.when(t < n_used)
    def _compute():
        slot = state[1]
        weights = [wb.at[slot] for wb in w_buf]
        n_valid = nv_ref[t]
        tile = a_ref.shape[0]
        for rows in range(MOE_ROW_STEP, tile + 1, MOE_ROW_STEP):
            @pl.when((n_valid > rows - MOE_ROW_STEP) & (n_valid <= rows))
            def _leading_rows(rows=rows):
                compute(a_ref.at[pl.ds(0, rows)], *weights, o_ref.at[pl.ds(0, rows)])
                if rows < tile:
                    o_ref[pl.ds(rows, tile - rows), :] = jnp.zeros((tile - rows, tn), o_ref.dtype)

    @pl.when(t >= n_used)
    def _unused_tile():
        o_ref[...] = jnp.zeros(o_ref.shape, o_ref.dtype)


def _grouped(compute, a, ws, l, tile_expert, n_tiles, tile_valid, tile, tn, out_dtype, name):
    R, K = a.shape
    n_cols = ws[0].shape[-1]
    max_tiles = R // tile
    n_w = len(ws)
    assert tile % MOE_ROW_STEP == 0
    grid_spec = pltpu.PrefetchScalarGridSpec(
        num_scalar_prefetch=3,
        grid=(n_cols // tn, max_tiles),
        in_specs=[pl.BlockSpec((tile, K),
                               lambda j, t, te, nt, nv: (jnp.minimum(t, nt[0] - 1), 0))]
        + [pl.BlockSpec(memory_space=pl.ANY)] * n_w,
        out_specs=pl.BlockSpec((tile, tn), lambda j, t, te, nt, nv: (t, j)),
        scratch_shapes=[pltpu.VMEM((2, K, tn), ws[0].dtype) for _ in range(n_w)]
        + [pltpu.SemaphoreType.DMA((n_w, 2)), pltpu.SMEM((2,), I32)],
    )
    return pl.pallas_call(
        functools.partial(_grouped_kernel, n_w=n_w, layer=l, tn=tn, max_tiles=max_tiles,
                          compute=compute),
        grid_spec=grid_spec,
        out_shape=jax.ShapeDtypeStruct((R, n_cols), out_dtype),
        compiler_params=_params(
            ("arbitrary", "arbitrary"),
            _dense_vmem(tile, K, tn, n_w, ws[0].dtype.itemsize,
                        jnp.dtype(out_dtype).itemsize, 2)),
        name=name,
    )(tile_expert, n_tiles, tile_valid, a, *ws)


def _combine_kernel(dest_ref, ys_hbm, wts_ref, x_ref, g_ref, gate_ref, o_ref,
                    stage, sems, *, tile, tiles_per_seq, n_steps):
    t = pl.program_id(0) * tiles_per_seq + pl.program_id(1)
    slot = t % 2

    def issue_tile(tt, to_slot):
        n0 = tt * tile

        def issue(rr, carry):
            a = (n0 + rr) * TOP_K
            for k in range(TOP_K):
                pltpu.make_async_copy(ys_hbm.at[pl.ds(dest_ref[a + k], 1), :],
                                      stage.at[to_slot, k, pl.ds(rr, 1), :],
                                      sems.at[to_slot, k]).start()
            return carry
        lax.fori_loop(0, tile, issue, 0, unroll=4)

    @pl.when(t == 0)
    def _first_tile():
        issue_tile(0, 0)

    @pl.when(t + 1 < n_steps)
    def _prefetch_next():
        issue_tile(t + 1, 1 - slot)

    for k in range(TOP_K):
        pltpu.make_async_copy(ys_hbm.at[pl.ds(0, tile), :], stage.at[slot, k],
                              sems.at[slot, k]).wait()
    w = wts_ref[...]
    y = w[:, 0:1] * stage[slot, 0] + w[:, 1:2] * stage[slot, 1]
    o_ref[...] = x_ref[...] + gate_ref[...] * _rms(y, g_ref[...])


def _combine(dest_flat, ys, wts3, x, gains, mod, l, g_idx, gate_idx):
    B, S, D = x.shape
    tile = min(S, 256)
    row = pl.BlockSpec((None, tile, D), lambda b, i, dest: (b, i, 0))
    grid_spec = pltpu.PrefetchScalarGridSpec(
        num_scalar_prefetch=1,
        grid=(B, S // tile),
        in_specs=[pl.BlockSpec(memory_space=pl.ANY),
                  pl.BlockSpec((None, tile, LANES), lambda b, i, dest: (b, i, 0)),
                  row,
                  pl.BlockSpec((None, None, 1, D), lambda b, i, dest: (l, g_idx, 0, 0)),
                  pl.BlockSpec((None, None, None, 1, D),
                               lambda b, i, dest: (l, b, gate_idx, 0, 0))],
        out_specs=row,
        scratch_shapes=[pltpu.VMEM((2, TOP_K, tile, D), F32),
                        pltpu.SemaphoreType.DMA((2, TOP_K))],
    )
    return pl.pallas_call(
        functools.partial(_combine_kernel, tile=tile, tiles_per_seq=S // tile,
                          n_steps=B * (S // tile)),
        grid_spec=grid_spec,
        out_shape=jax.ShapeDtypeStruct((B, S, D), F32),
        compiler_params=_params(("arbitrary", "arbitrary"), 12 * tile * D * 4),
        name="moe_combine",
    )(dest_flat, ys, wts3, x, gains, mod)


def kernel(x, c, norm_gains, w_mod, b_mod, w_in, b_forget, w_pool, pool_scale, w_out,
           w_ffn_gate, w_ffn_up, w_ffn_down, w_router, w_exp_gate, w_exp_up, w_exp_down):
    B, S, D = x.shape
    L = w_mod.shape[0]
    N = B * S
    H = b_forget.shape[1]
    A = H * HEAD_DIM
    P = pool_scale.shape[1]
    assert A + P == D and w_in.shape[2] == 3 * A + H + P and H <= LANES
    assert P % LANES == 0 and A % LANES == 0
    E = w_router.shape[2]
    tm = min(N, 1024)
    tm_wide = min(N, 2048)
    tn_fu = 2 * LANES
    fu_cols = -(-(P + LANES) // tn_fu) * tn_fu

    mod = _mod(c, w_mod, b_mod)
    gains = norm_gains.reshape(L, 4, 1, D)
    h = _norm_mod(x, gains, mod, 0, 0, 0, 1, BF16)

    qk_scale = jnp.concatenate([jnp.full((1, A), HEAD_DIM ** -0.5 * LOG2_E, F32),
                                jnp.ones((1, 2 * A), F32)], axis=1)
    pool_scale3 = pool_scale.reshape(L, 1, P)
    w_in_t = jnp.swapaxes(w_in, 1, 2)

    for l in range(L):
        h2 = h.reshape(N, D)
        qkv_h = _qkv_proj(h2, w_in_t, l, 3 * A, tm_wide, min(3 * A, 512), qk_scale)
        w_fu_t = jnp.concatenate([w_in_t[l, 3 * A + H:], w_in_t[l, 3 * A:3 * A + H],
                                  jnp.zeros((fu_cols - P - H, D), F32)], axis=0)
        fu = _matmul_nt(h2, w_fu_t, tm, tn_fu, F32, "in_proj_gate_pool")
        fu3 = fu.reshape(B, S, fu_cols)
        b_pad = jnp.pad(b_forget[l], (0, LANES - H)).reshape(1, LANES)
        f_sh, f_hs = _fox_prep(fu3, b_pad, P // LANES)
        attn = _attention(qkv_h.reshape(3 * H, B, S, HEAD_DIM), f_sh,
                          f_hs.reshape(B, LANES, 1, S), H)
        pooled = _pool(fu3, w_pool, pool_scale3, l)
        mixed = _matmul_two(attn.reshape(N, A), pooled.reshape(N, P), w_out, (l,), tm_wide,
                            min(D, 512), F32, "out_proj")

        i = l // 2
        if l % 2 == 0:
            x, h = _resid_norm(x, mixed.reshape(B, S, D), gains, mod, l, 1, 2,
                               nxt=(l, 2, 3, 4))
            FF = w_ffn_gate.shape[2]
            tn_ff = 256 if FF % 256 == 0 else LANES
            act = _gateup(h.reshape(N, D), w_ffn_gate, w_ffn_up, (i,), tm_wide, tn_ff)
            y = _down(act, w_ffn_down, (i,), tm, min(D, 256))
            y3 = y.reshape(B, S, D)
            if l + 1 < L:
                x, h = _resid_norm(x, y3, gains, mod, l, 3, 5, nxt=(l + 1, 0, 0, 1))
            else:
                (x,) = _resid_norm(x, y3, gains, mod, l, 3, 5)
        else:
            wr_pad = jnp.pad(w_router[i], ((0, 0), (0, LANES - E)))
            x, hf, logits = _resid_norm(x, mixed.reshape(B, S, D), gains, mod, l, 1, 2,
                                        nxt=(l, 2, 3, 4), h_dtype=F32, w_router=wr_pad)
            tile = min(N, 512)
            max_tiles = (N * TOP_K) // tile + E
            assert max_tiles <= LANES
            dest, wts, meta = _route(logits.reshape(N, LANES), E, tile)
            dest_flat = dest[:, :TOP_K].reshape(N * TOP_K)
            tile_expert = meta[0, :max_tiles]
            n_tiles = meta[1, :1]
            tile_valid = meta[2, :max_tiles]
            xs = _gather(dest_flat, n_tiles, hf.reshape(N, D), tile, max_tiles)
            tn_e = min(w_exp_gate.shape[3], 512)
            act = _grouped(_gateup_kernel, xs, (w_exp_gate, w_exp_up), i,
                           tile_expert, n_tiles, tile_valid, tile, tn_e, BF16, "moe_gate_up")
            ys = _grouped(_mm_kernel, act, (w_exp_down,), i,
                          tile_expert, n_tiles, tile_valid, tile, min(D, 1024), F32, "moe_down")
            x = _combine(dest_flat, ys, wts.reshape(B, S, LANES), x, gains, mod, l, 3, 5)
            if l + 1 < L:
                h = _norm_mod(x, gains, mod, l + 1, 0, 0, 1, BF16)
    return x
```

```python
import functools

import jax
import jax.numpy as jnp
from jax import lax
from jax.experimental import pallas as pl
from jax.experimental.pallas import tpu as pltpu

F32 = jnp.float32
BF16 = jnp.bfloat16
I32 = jnp.int32
U32 = jnp.uint32

HEAD_DIM = 128
POOL_WINDOWS = (2, 4, 8, 16)
N_MOD = 6
TOP_K = 2
RMS_EPS = 1e-6
LOG2_E = 1.4426950408889634
MOE_ROW_STEP = 128

LANES = 128
SUBLANES = 8
V7X_VMEM_BYTES = 64 * 2**20
VMEM_CAP_BYTES = V7X_VMEM_BYTES - 6 * 2**20


def _params(semantics, vmem_bytes):
    limit = int(min(max(vmem_bytes * 5 // 4, 16 * 2**20), VMEM_CAP_BYTES))
    return pltpu.CompilerParams(dimension_semantics=semantics, vmem_limit_bytes=limit)


def _rms(x, gain):
    inv = lax.rsqrt(jnp.mean(x * x, axis=-1, keepdims=True) + RMS_EPS)
    return x * inv * gain


def _mod_kernel(c_ref, w_ref, b_ref, o_ref):
    c = c_ref[...]
    c_act = (c * jax.nn.sigmoid(c)).astype(BF16)
    o_ref[...] = jnp.dot(c_act, w_ref[...].astype(BF16),
                         preferred_element_type=F32) + b_ref[...]


def _mod(c, w_mod, b_mod):
    B, D = c.shape
    L, _, NM = w_mod.shape
    rows = -(-B // SUBLANES) * SUBLANES
    c_pad = jnp.pad(c, ((0, rows - B), (0, 0)))
    tn = 512
    out = pl.pallas_call(
        _mod_kernel,
        grid=(L, NM // tn),
        in_specs=[
            pl.BlockSpec((rows, D), lambda l, j: (0, 0)),
            pl.BlockSpec((None, D, tn), lambda l, j: (l, 0, j)),
            pl.BlockSpec((None, 1, tn), lambda l, j: (l, 0, j)),
        ],
        out_specs=pl.BlockSpec((None, rows, tn), lambda l, j: (l, 0, j)),
        out_shape=jax.ShapeDtypeStruct((L, rows, NM), F32),
        compiler_params=_params(("arbitrary", "arbitrary"), 3 * D * tn * 4),
        name="adaln_mod",
    )(c_pad, w_mod, b_mod.reshape(L, 1, NM))
    return out[:, :B].reshape(L, B, N_MOD, 1, D)


def _mod_spec(l, idx, D):
    return pl.BlockSpec((None, None, None, 1, D), lambda b, i: (l, b, idx, 0, 0))


def _gain_spec(l, idx, D):
    return pl.BlockSpec((None, None, 1, D), lambda b, i: (l, idx, 0, 0))


def _norm_mod_kernel(x_ref, g_ref, sc_ref, sh_ref, o_ref):
    h = _rms(x_ref[...], g_ref[...])
    o_ref[...] = (h * (1.0 + sc_ref[...]) + sh_ref[...]).astype(o_ref.dtype)


def _norm_mod(x, gains, mod, l, gain_idx, shift_idx, scale_idx, out_dtype):
    B, S, D = x.shape
    ts = min(S, 256)
    row = pl.BlockSpec((None, ts, D), lambda b, i: (b, i, 0))
    return pl.pallas_call(
        _norm_mod_kernel,
        grid=(B, S // ts),
        in_specs=[row, _gain_spec(l, gain_idx, D), _mod_spec(l, scale_idx, D),
                  _mod_spec(l, shift_idx, D)],
        out_specs=row,
        out_shape=jax.ShapeDtypeStruct((B, S, D), out_dtype),
        compiler_params=_params(("arbitrary", "arbitrary"), 4 * ts * D * 4),
        name="pre_norm",
    )(x, gains, mod, mod)


def _split_bf16(v):
    hi = v.astype(BF16)
    lo = (v - hi.astype(F32)).astype(BF16)
    return hi, lo


def _pack_bf16_halves(v):
    half = v.shape[-1] // 2
    bits = lax.bitcast_convert_type(v.astype(BF16).astype(F32), U32)
    return bits[:, half:] | (bits[:, :half] >> 16)


def _unpack_bf16_halves(words):
    lo = lax.bitcast_convert_type(words << 16, F32)
    hi = lax.bitcast_convert_type(words & jnp.uint32(0xFFFF0000), F32)
    return lo.astype(BF16), hi.astype(BF16)


def _resid_norm_kernel(*refs, emit_h, emit_logits):
    x_ref, y_ref, ga_ref, gate_ref = refs[:4]
    pos = 4
    if emit_h:
        gb_ref, sc_ref, sh_ref = refs[pos:pos + 3]
        pos += 3
    if emit_logits:
        wr_ref = refs[pos]
        pos += 1
    xo_ref = refs[pos]
    pos += 1
    x_new = x_ref[...] + gate_ref[...] * _rms(y_ref[...], ga_ref[...])
    xo_ref[...] = x_new
    if emit_h:
        h_ref = refs[pos]
        pos += 1
        h = _rms(x_new, gb_ref[...]) * (1.0 + sc_ref[...]) + sh_ref[...]
        if h_ref.dtype == U32:
            h_ref[...] = _pack_bf16_halves(h)
        else:
            h_ref[...] = h.astype(h_ref.dtype)
        if emit_logits:
            lg_ref = refs[pos]
            h_hi, h_lo = _split_bf16(h)
            w_hi, w_lo = _split_bf16(wr_ref[...])
            lg_ref[...] = (jnp.dot(h_hi, w_hi, preferred_element_type=F32)
                           + jnp.dot(h_lo, w_hi, preferred_element_type=F32)
                           + jnp.dot(h_hi, w_lo, preferred_element_type=F32))


def _resid_norm(x, y, gains, mod, la, ga_idx, gate_idx, nxt=None, h_dtype=BF16,
                w_router=None):
    B, S, D = x.shape
    ts = min(S, 256)
    row = pl.BlockSpec((None, ts, D), lambda b, i: (b, i, 0))
    in_specs = [row, row, _gain_spec(la, ga_idx, D), _mod_spec(la, gate_idx, D)]
    args = [x, y, gains, mod]
    out_specs = [row]
    out_shape = [jax.ShapeDtypeStruct((B, S, D), F32)]
    emit_h = nxt is not None
    emit_logits = w_router is not None
    if emit_h:
        lb, gb_idx, shift_idx, scale_idx = nxt
        in_specs += [_gain_spec(lb, gb_idx, D), _mod_spec(lb, scale_idx, D),
                     _mod_spec(lb, shift_idx, D)]
        args += [gains, mod, mod]
        h_cols = D // 2 if h_dtype == U32 else D
        out_specs.append(pl.BlockSpec((None, ts, h_cols), lambda b, i: (b, i, 0)))
        out_shape.append(jax.ShapeDtypeStruct((B, S, h_cols), h_dtype))
    if emit_logits:
        in_specs.append(pl.BlockSpec((D, LANES), lambda b, i: (0, 0)))
        args.append(w_router)
        out_specs.append(pl.BlockSpec((None, ts, LANES), lambda b, i: (b, i, 0)))
        out_shape.append(jax.ShapeDtypeStruct((B, S, LANES), F32))
    outs = pl.pallas_call(
        functools.partial(_resid_norm_kernel, emit_h=emit_h, emit_logits=emit_logits),
        grid=(B, S // ts),
        in_specs=in_specs,
        out_specs=out_specs,
        out_shape=out_shape,
        compiler_params=_params(("arbitrary", "arbitrary"), 10 * ts * D * 4),
        name="resid_norm",
    )(*args)
    return outs


def _mm_kernel(a_ref, w_ref, o_ref):
    o_ref[...] = jnp.dot(a_ref[...], w_ref[...].astype(BF16),
                         preferred_element_type=F32).astype(o_ref.dtype)


def _nt_dot(a, wt):
    return lax.dot_general(a, wt, (((1,), (1,)), ((), ())), preferred_element_type=F32)


def _mm_nt_kernel(a_ref, wt_ref, o_ref):
    o_ref[...] = _nt_dot(a_ref[...], wt_ref[...].astype(BF16)).astype(o_ref.dtype)


def _qkv_kernel(a_ref, wt_ref, cs_ref, o_ref):
    acc = _nt_dot(a_ref[...], wt_ref[...].astype(BF16)) * cs_ref[...]
    for hh in range(o_ref.shape[0]):
        o_ref[hh] = acc[:, hh * HEAD_DIM:(hh + 1) * HEAD_DIM].astype(o_ref.dtype)


def _mm_two_kernel(a1_ref, a2_ref, w_ref, o_ref):
    k1 = a1_ref.shape[1]
    w = w_ref[...].astype(BF16)
    o_ref[...] = (jnp.dot(a1_ref[...], w[:k1], preferred_element_type=F32)
                  + jnp.dot(a2_ref[...], w[k1:], preferred_element_type=F32)).astype(o_ref.dtype)


def _gateup_kernel(a_ref, wg_ref, wu_ref, o_ref):
    a = a_ref[...]
    g = jnp.dot(a, wg_ref[...].astype(BF16), preferred_element_type=F32)
    u = jnp.dot(a, wu_ref[...].astype(BF16), preferred_element_type=F32)
    o_ref[...] = (g * jax.nn.sigmoid(g) * u).astype(o_ref.dtype)


def _w_spec(w, tn, lead):
    K = w.shape[-2]
    nlead = len(lead)
    return pl.BlockSpec((None,) * nlead + (K, tn), lambda i, j: tuple(lead) + (0, j))


def _a_spec(tm, K):
    return pl.BlockSpec((tm, K), lambda i, j: (i, 0), pipeline_mode=pl.Buffered(1))


def _dense_vmem(tm, K, tn, n_w, w_bytes, out_bytes, a_buffers):
    a = a_buffers * tm * K * 2
    w = n_w * (2 * K * tn * w_bytes + K * tn * 2)
    o = 2 * tm * tn * out_bytes + n_w * tm * tn * 4
    return a + w + o


def _qkv_proj(a, wt, l, n_cols, tm, tn, colscale):
    M, K = a.shape
    hpb = tn // HEAD_DIM
    return pl.pallas_call(
        _qkv_kernel,
        grid=(M // tm, n_cols // tn),
        in_specs=[_a_spec(tm, K),
                  pl.BlockSpec((None, tn, K), lambda i, j: (l, j, 0)),
                  pl.BlockSpec((1, tn), lambda i, j: (0, j))],
        out_specs=pl.BlockSpec((hpb, tm, HEAD_DIM), lambda i, j: (j, i, 0)),
        out_shape=jax.ShapeDtypeStruct((n_cols // HEAD_DIM, M, HEAD_DIM), BF16),
        compiler_params=_params(("arbitrary", "arbitrary"),
                                _dense_vmem(tm, K, tn, 1, wt.dtype.itemsize, 2, 1)),
        name="in_proj_qkv",
    )(a, wt, colscale)


def _matmul_nt(a, wt, tm, tn, out_dtype, name):
    M, K = a.shape
    n_cols = wt.shape[0]
    return pl.pallas_call(
        _mm_nt_kernel,
        grid=(M // tm, n_cols // tn),
        in_specs=[pl.BlockSpec((tm, K), lambda i, j: (i, 0)),
                  pl.BlockSpec((tn, K), lambda i, j: (j, 0))],
        out_specs=pl.BlockSpec((tm, tn), lambda i, j: (i, j)),
        out_shape=jax.ShapeDtypeStruct((M, n_cols), out_dtype),
        compiler_params=_params(
            ("arbitrary", "arbitrary"),
            _dense_vmem(tm, K, tn, 1, wt.dtype.itemsize, jnp.dtype(out_dtype).itemsize, 2)),
        name=name,
    )(a, wt)


def _matmul_two(a1, a2, w, lead, tm, tn, out_dtype, name):
    M, K1 = a1.shape
    K2 = a2.shape[1]
    n_cols = w.shape[-1]
    return pl.pallas_call(
        _mm_two_kernel,
        grid=(M // tm, n_cols // tn),
        in_specs=[_a_spec(tm, K1), _a_spec(tm, K2), _w_spec(w, tn, lead)],
        out_specs=pl.BlockSpec((tm, tn), lambda i, j: (i, j)),
        out_shape=jax.ShapeDtypeStruct((M, n_cols), out_dtype),
        compiler_params=_params(
            ("arbitrary", "arbitrary"),
            _dense_vmem(tm, K1 + K2, tn, 1, w.dtype.itemsize, jnp.dtype(out_dtype).itemsize, 1)),
        name=name,
    )(a1, a2, w)


def _gateup(a, wg, wu, lead, tm, tn, name="ffn_gate_up"):
    M, K = a.shape
    n_cols = wg.shape[-1]
    return pl.pallas_call(
        _gateup_kernel,
        grid=(M // tm, n_cols // tn),
        in_specs=[_a_spec(tm, K), _w_spec(wg, tn, lead), _w_spec(wu, tn, lead)],
        out_specs=pl.BlockSpec((tm, tn), lambda i, j: (i, j)),
        out_shape=jax.ShapeDtypeStruct((M, n_cols), BF16),
        compiler_params=_params(("arbitrary", "arbitrary"),
                                _dense_vmem(tm, K, tn, 2, wg.dtype.itemsize, 2, 1)),
        name=name,
    )(a, wg, wu)


def _down(a, w, lead, tm, tn, name="ffn_down"):
    M, K = a.shape
    n_cols = w.shape[-1]
    return pl.pallas_call(
        _mm_kernel,
        grid=(M // tm, n_cols // tn),
        in_specs=[_a_spec(tm, K), _w_spec(w, tn, lead)],
        out_specs=pl.BlockSpec((tm, tn), lambda i, j: (i, j)),
        out_shape=jax.ShapeDtypeStruct((M, n_cols), F32),
        compiler_params=_params(("arbitrary", "arbitrary"),
                                _dense_vmem(tm, K, tn, 1, w.dtype.itemsize, 4, 1)),
        name=name,
    )(a, w)


def _fox_prep_kernel(f_ref, b_ref, fsh_ref, fhs_ref):
    z = f_ref[...] + b_ref[...]
    acc = jnp.minimum(z, 0.0) - jnp.log1p(jnp.exp(-jnp.abs(z)))
    S = acc.shape[0]
    row = lax.broadcasted_iota(I32, acc.shape, 0)
    d = 1
    while d < S:
        acc = acc + jnp.where(row >= d, pltpu.roll(acc, d, 0), 0.0)
        d *= 2
    acc = acc * LOG2_E
    fsh_ref[...] = acc
    fhs_ref[...] = acc.T


def _fox_prep(fu3, b_pad, col_block):
    B, S, _ = fu3.shape
    return pl.pallas_call(
        _fox_prep_kernel,
        grid=(B,),
        in_specs=[pl.BlockSpec((None, S, LANES), lambda b: (b, 0, col_block)),
                  pl.BlockSpec((1, LANES), lambda b: (0, 0))],
        out_specs=[pl.BlockSpec((None, S, LANES), lambda b: (b, 0, 0)),
                   pl.BlockSpec((None, LANES, S), lambda b: (b, 0, 0))],
        out_shape=[jax.ShapeDtypeStruct((B, S, LANES), F32),
                   jax.ShapeDtypeStruct((B, LANES, S), F32)],
        compiler_params=_params(("arbitrary",), 12 * S * LANES * 4),
        name="fox_prep",
    )(fu3, b_pad)


def _attn_kernel(q_ref, k_ref, v_ref, fq_ref, fk_ref, o_ref, fq_s, *, blk):
    heads = q_ref.shape[0]
    for hh in range(heads):
        _attn_head(pl.program_id(1) * heads + hh, q_ref.at[hh], k_ref.at[hh], v_ref.at[hh],
                   fq_ref, fk_ref.at[hh], o_ref, hh * HEAD_DIM, fq_s.at[hh], blk)


def _attn_head(h, q_ref, k_ref, v_ref, fq_ref, fk_ref, o_ref, o_col, fq_s, blk):
    S = q_ref.shape[0]
    n_blk = S // blk
    lane = lax.broadcasted_iota(I32, fq_ref.shape, 1)
    fq_col = jnp.sum(jnp.where(lane == h, fq_ref[...], 0.0), axis=-1, keepdims=True)
    fq_s[...] = jnp.broadcast_to(fq_col, fq_s.shape)

    def logits(row_lo, n_rows, key_lo, n_keys):
        q = q_ref[row_lo:row_lo + n_rows, :]
        fq = fq_s[row_lo:row_lo + n_rows, :]
        s = lax.dot_general(q, k_ref[key_lo:key_lo + n_keys, :], (((1,), (1,)), ((), ())),
                            preferred_element_type=F32)
        chunks = []
        for c in range(n_keys // LANES):
            lo = key_lo + c * LANES
            sc = s[:, c * LANES:(c + 1) * LANES] + (fq - fk_ref[:, lo:lo + LANES])
            if lo + LANES - 1 > row_lo:
                r = lax.broadcasted_iota(I32, (n_rows, LANES), 0) + row_lo
                k = lax.broadcasted_iota(I32, (n_rows, LANES), 1) + lo
                sc = jnp.where(k <= r, sc, -jnp.inf)
            chunks.append(sc)
        return chunks

    def online_update(state, chunks, v):
        m_row = jnp.max(functools.reduce(jnp.maximum, chunks), axis=-1, keepdims=True)
        m_new = m_row if state is None else jnp.maximum(state[0], m_row)
        m_new = jnp.broadcast_to(m_new, chunks[0].shape)
        ps = [jnp.exp2(sc - m_new) for sc in chunks]
        p_sum = functools.reduce(lambda a, b: a + b, ps)
        pv = jnp.dot(jnp.concatenate(ps, axis=1).astype(BF16), v, preferred_element_type=F32)
        if state is None:
            return m_new, p_sum, pv
        alpha = jnp.exp2(state[0] - m_new)
        return m_new, alpha * state[1] + p_sum, alpha * state[2] + pv

    for i in range(n_blk):
        r0 = i * blk
        state = None
        for j in range(i):
            k0 = j * blk
            state = online_update(state, logits(r0, blk, k0, blk), v_ref[k0:k0 + blk, :])
        _, l_b, acc = online_update(state, logits(r0, blk, r0, blk), v_ref[r0:r0 + blk, :])
        l_row = jnp.sum(l_b, axis=-1, keepdims=True)
        o_ref[r0:r0 + blk, o_col:o_col + HEAD_DIM] = (acc / l_row).astype(o_ref.dtype)


def _attention(qkv_h, f_sh, f_hs4, n_heads):
    _, B, S, _ = qkv_h.shape
    H = n_heads
    blk = min(S, 512)
    hps = 2 if H % 2 == 0 else 1
    head = lambda off: pl.BlockSpec((hps, None, S, HEAD_DIM),
                                    lambda b, g: (off // hps + g, b, 0, 0))
    return pl.pallas_call(
        functools.partial(_attn_kernel, blk=blk),
        grid=(B, H // hps),
        in_specs=[head(0), head(H), head(2 * H),
                  pl.BlockSpec((None, S, LANES), lambda b, g: (b, 0, 0)),
                  pl.BlockSpec((None, hps, 1, S), lambda b, g: (b, g, 0, 0))],
        out_specs=pl.BlockSpec((None, S, hps * HEAD_DIM), lambda b, g: (b, 0, g)),
        out_shape=jax.ShapeDtypeStruct((B, S, H * HEAD_DIM), BF16),
        scratch_shapes=[pltpu.VMEM((hps, S, LANES), F32)],
        compiler_params=_params(
            ("arbitrary", "arbitrary"),
            hps * (8 * S * HEAD_DIM * 2 + 3 * S * LANES * 4 + 8 * blk * blk * 4)),
        name="fox_attention",
    )(qkv_h, qkv_h, qkv_h, f_sh, f_hs4)


def _pool_kernel(u_ref, w_ref, sc_ref, o_ref):
    g = pl.program_id(1)
    u = u_ref[...]
    row = lax.broadcasted_iota(I32, u.shape, 0)

    def shifted(v, d):
        return jnp.where(row >= d, pltpu.roll(v, d, 0), 0.0)

    s2 = u + shifted(u, 1)
    s4 = s2 + shifted(s2, 2)
    s8 = s4 + shifted(s4, 4)
    s16 = s8 + shifted(s8, 8)
    win = jnp.where(g == 0, s2, jnp.where(g == 1, s4, jnp.where(g == 2, s8, s16)))
    width = lax.shift_left(jnp.int32(POOL_WINDOWS[0]), g)
    count = jnp.minimum(row + 1, width).astype(F32)
    pooled = win / count - u
    y = jnp.dot(pooled.astype(BF16), w_ref[...].astype(BF16), preferred_element_type=F32)
    o_ref[...] = (y * sc_ref[...]).astype(o_ref.dtype)


def _pool(fu3, w_pool, pool_scale3, l):
    B, S, _ = fu3.shape
    G, C = w_pool.shape[1], w_pool.shape[2]
    assert G == len(POOL_WINDOWS) and all(
        w == POOL_WINDOWS[0] << k for k, w in enumerate(POOL_WINDOWS))
    return pl.pallas_call(
        _pool_kernel,
        grid=(B, G),
        in_specs=[pl.BlockSpec((None, S, C), lambda b, g: (b, 0, g)),
                  pl.BlockSpec((None, None, C, C), lambda b, g: (l, g, 0, 0)),
                  pl.BlockSpec((None, 1, C), lambda b, g: (l, 0, g))],
        out_specs=pl.BlockSpec((None, S, C), lambda b, g: (b, 0, g)),
        out_shape=jax.ShapeDtypeStruct((B, S, G * C), BF16),
        compiler_params=_params(("arbitrary", "arbitrary"), 14 * S * C * 4),
        name="pool_mixer",
    )(fu3, w_pool, pool_scale3)


def _route_kernel(lg_ref, dest_ref, wts_ref, meta_ref, m_all, m_top, tri, *,
                  n_experts, tile, chunk):
    N = lg_ref.shape[0]
    n_chunks = N // chunk
    lane = lax.broadcasted_iota(I32, (chunk, LANES), 1).astype(F32)
    lane1 = lax.broadcasted_iota(I32, (1, LANES), 1).astype(F32)
    r = lax.broadcasted_iota(I32, (chunk, chunk), 0)
    c = lax.broadcasted_iota(I32, (chunk, chunk), 1)
    tri[...] = jnp.where(c < r, 1.0, 0.0).astype(BF16)

    def select(ci, counts):
        rows = pl.ds(pl.multiple_of(ci * chunk, chunk), chunk)
        lg = jnp.where(lane < n_experts, lg_ref[rows, :], -jnp.inf)
        v1 = jnp.max(lg, axis=-1, keepdims=True)
        i1 = jnp.min(jnp.where(lg == v1, lane, float(LANES)), axis=-1, keepdims=True)
        first = lane == i1
        lg2 = jnp.where(first, -jnp.inf, lg)
        v2 = jnp.max(lg2, axis=-1, keepdims=True)
        i2 = jnp.min(jnp.where(lg2 == v2, lane, float(LANES)), axis=-1, keepdims=True)
        second = lane == i2
        e2 = jnp.exp(v2 - v1)
        w1 = 1.0 / (1.0 + e2)
        w2 = e2 / (1.0 + e2)
        wts_ref[rows, :] = jnp.where(lane == 0.0, w1, jnp.where(lane == 1.0, w2, 0.0))
        both = jnp.where(first | second, 1.0, 0.0)
        m_all[rows, :] = both.astype(BF16)
        m_top[rows, :] = jnp.where(first, 1.0, 0.0).astype(BF16)
        return counts + jnp.sum(both, axis=0, keepdims=True)

    counts = lax.fori_loop(0, n_chunks, select, jnp.zeros((1, LANES), F32))
    padded = jnp.ceil(counts / tile) * tile
    start = jnp.zeros((1, LANES), F32)
    tile_row = lane1 * tile
    tile_expert = jnp.zeros((1, LANES), F32)
    last_used = jnp.zeros((1, 1), F32)
    running = jnp.zeros((1, 1), F32)
    tile_valid = jnp.zeros((1, LANES), F32)
    for e in range(n_experts):
        rows_e = jnp.sum(jnp.where(lane1 == e, padded, 0.0), axis=-1, keepdims=True)
        count_e = jnp.sum(jnp.where(lane1 == e, counts, 0.0), axis=-1, keepdims=True)
        in_group = (tile_row >= running) & (tile_row < running + rows_e)
        tile_valid = jnp.where(in_group, jnp.minimum(running + count_e - tile_row, float(tile)),
                               tile_valid)
        start = start + jnp.where(lane1 > e, rows_e, 0.0)
        running = running + rows_e
        tile_expert = tile_expert + jnp.where(tile_row >= running, 1.0, 0.0)
        last_used = jnp.where(rows_e > 0.0, float(e), last_used)
    tile_expert = jnp.minimum(tile_expert, last_used)
    n_tiles = running / tile
    sub = lax.broadcasted_iota(I32, (SUBLANES, LANES), 0)
    meta_ref[...] = jnp.where(sub == 0, tile_expert,
                              jnp.where(sub == 1, n_tiles,
                                        jnp.where(sub == 2, tile_valid, 0.0))).astype(I32)

    def place(ci, carry):
        rows = pl.ds(pl.multiple_of(ci * chunk, chunk), chunk)
        both = m_all[rows, :]
        top = m_top[rows, :].astype(F32)
        rank = jnp.dot(tri[...], both, preferred_element_type=F32)
        pos = start + carry + rank
        both_f = both.astype(F32)
        d1 = jnp.sum(top * pos, axis=-1, keepdims=True)
        d2 = jnp.sum((both_f - top) * pos, axis=-1, keepdims=True)
        dest_ref[rows, :] = jnp.where(lane == 0.0, d1, jnp.where(lane == 1.0, d2, 0.0)).astype(I32)
        return carry + jnp.sum(both_f, axis=0, keepdims=True)

    lax.fori_loop(0, n_chunks, place, jnp.zeros((1, LANES), F32))


def _route(logits, n_experts, tile):
    N = logits.shape[0]
    chunk = min(N, 512)
    return pl.pallas_call(
        functools.partial(_route_kernel, n_experts=n_experts, tile=tile, chunk=chunk),
        out_shape=[jax.ShapeDtypeStruct((N, LANES), I32),
                   jax.ShapeDtypeStruct((N, LANES), F32),
                   jax.ShapeDtypeStruct((SUBLANES, LANES), I32)],
        scratch_shapes=[pltpu.VMEM((N, LANES), BF16), pltpu.VMEM((N, LANES), BF16),
                        pltpu.VMEM((chunk, chunk), BF16)],
        compiler_params=pltpu.CompilerParams(
            vmem_limit_bytes=int(min(max(N * LANES * 4 * 8, 16 * 2**20), VMEM_CAP_BYTES))),
        name="moe_route",
    )(logits)


def _gather_kernel(dest_ref, nt_ref, h_hbm, o_ref, src, stage, sems, *, tile, n_rows, n_assign):
    t = pl.program_id(0)
    n_used = nt_ref[0]
    slot = t % 2

    def issue_tile(tt, to_slot):
        base = tt * tile

        def issue(rr, carry):
            tok = src[base + rr]
            pltpu.make_async_copy(h_hbm.at[pl.ds(tok, 1), :],
                                  stage.at[to_slot, pl.ds(rr, 1), :], sems.at[to_slot]).start()
            return carry
        lax.fori_loop(0, tile, issue, 0, unroll=8)

    @pl.when(t == 0)
    def _invert_dest():
        def clear(rr, carry):
            src[rr] = 0
            return carry
        lax.fori_loop(0, n_rows, clear, 0, unroll=8)

        def scatter(n, carry):
            for k in range(TOP_K):
                src[dest_ref[n * TOP_K + k]] = n
            return carry
        lax.fori_loop(0, n_assign // TOP_K, scatter, 0, unroll=4)
        issue_tile(0, 0)

    @pl.when(t + 1 < n_used)
    def _prefetch_next():
        issue_tile(t + 1, 1 - slot)

    @pl.when(t < n_used)
    def _gather():
        pltpu.make_async_copy(h_hbm.at[pl.ds(0, tile), :], stage.at[slot], sems.at[slot]).wait()
        half = stage.shape[-1]
        lo, hi = _unpack_bf16_halves(stage[slot])
        o_ref[:, :half] = lo
        o_ref[:, half:] = hi

    @pl.when(t >= n_used)
    def _unused_tile():
        o_ref[...] = jnp.zeros(o_ref.shape, o_ref.dtype)


def _gather(dest_flat, n_tiles, h_packed, tile, max_tiles):
    N, half = h_packed.shape
    D = 2 * half
    n_rows = max_tiles * tile
    grid_spec = pltpu.PrefetchScalarGridSpec(
        num_scalar_prefetch=2,
        grid=(max_tiles,),
        in_specs=[pl.BlockSpec(memory_space=pl.ANY)],
        out_specs=pl.BlockSpec((tile, D), lambda t, dest, nt: (t, 0)),
        scratch_shapes=[pltpu.SMEM((n_rows,), I32), pltpu.VMEM((2, tile, half), U32),
                        pltpu.SemaphoreType.DMA((2,))],
    )
    return pl.pallas_call(
        functools.partial(_gather_kernel, tile=tile, n_rows=n_rows, n_assign=dest_flat.shape[0]),
        grid_spec=grid_spec,
        out_shape=jax.ShapeDtypeStruct((n_rows, D), BF16),
        compiler_params=_params(("arbitrary",), 4 * tile * D * 4),
        name="moe_gather",
    )(dest_flat, n_tiles, h_packed)


def _grouped_kernel(te_ref, nt_ref, nv_ref, a_ref, *rest, n_w, layer, tn, max_tiles, compute):
    w_hbm = rest[:n_w]
    o_ref = rest[n_w]
    w_buf = rest[n_w + 1:2 * n_w + 1]
    sems, state = rest[2 * n_w + 1:]
    j = pl.program_id(0)
    t = pl.program_id(1)
    n_used = nt_ref[0]

    def fetch(e, jj, slot):
        return [pltpu.make_async_copy(w_hbm[wi].at[layer, e, :, pl.ds(jj * tn, tn)],
                                      w_buf[wi].at[slot], sems.at[wi, slot])
                for wi in range(n_w)]

    @pl.when((j == 0) & (t == 0))
    def _first_fetch():
        state[0] = 0
        for cp in fetch(te_ref[0], 0, 0):
            cp.start()

    e = te_ref[t]
    run_start = (t < n_used) & ((t == 0) | (e != te_ref[jnp.maximum(t - 1, 0)]))

    @pl.when(run_start)
    def _swap_weights():
        slot = state[0]
        for cp in fetch(e, j, slot):
            cp.wait()
        t_next = lax.while_loop(
            lambda tt: (tt < n_used) & (te_ref[jnp.minimum(tt, max_tiles - 1)] == e),
            lambda tt: tt + 1, t + 1)
        same_sweep = t_next < n_used

        @pl.when(same_sweep)
        def _next_expert():
            for cp in fetch(te_ref[jnp.minimum(t_next, max_tiles - 1)], j, 1 - slot):
                cp.start()

        @pl.when(jnp.logical_not(same_sweep) & (j + 1 < pl.num_programs(0)))
        def _next_sweep():
            for cp in fetch(te_ref[0], j + 1, 1 - slot):
                cp.start()

        state[1] = slot
        state[0] = 1 - slot

    @pl.when(t < n_used)
    def _compute():
        slot = state[1]
        weights = [wb.at[slot] for wb in w_buf]
        n_valid = nv_ref[t]
        tile = a_ref.shape[0]
        for rows in range(MOE_ROW_STEP, tile + 1, MOE_ROW_STEP):
            @pl.when((n_valid > rows - MOE_ROW_STEP) & (n_valid <= rows))
            def _leading_rows(rows=rows):
                compute(a_ref.at[pl.ds(0, rows)], *weights, o_ref.at[pl.ds(0, rows)])
                if rows < tile:
                    o_ref[pl.ds(rows, tile - rows), :] = jnp.zeros((tile - rows, tn), o_ref.dtype)

    @pl.when(t >= n_used)
    def _unused_tile():
        o_ref[...] = jnp.zeros(o_ref.shape, o_ref.dtype)


def _grouped(compute, a, ws, l, tile_expert, n_tiles, tile_valid, tile, tn, out_dtype, name):
    R, K = a.shape
    n_cols = ws[0].shape[-1]
    max_tiles = R // tile
    n_w = len(ws)
    assert tile % MOE_ROW_STEP == 0
    grid_spec = pltpu.PrefetchScalarGridSpec(
        num_scalar_prefetch=3,
        grid=(n_cols // tn, max_tiles),
        in_specs=[pl.BlockSpec((tile, K),
                               lambda j, t, te, nt, nv: (jnp.minimum(t, nt[0] - 1), 0))]
        + [pl.BlockSpec(memory_space=pl.ANY)] * n_w,
        out_specs=pl.BlockSpec((tile, tn), lambda j, t, te, nt, nv: (t, j)),
        scratch_shapes=[pltpu.VMEM((2, K, tn), ws[0].dtype) for _ in range(n_w)]
        + [pltpu.SemaphoreType.DMA((n_w, 2)), pltpu.SMEM((2,), I32)],
    )
    return pl.pallas_call(
        functools.partial(_grouped_kernel, n_w=n_w, layer=l, tn=tn, max_tiles=max_tiles,
                          compute=compute),
        grid_spec=grid_spec,
        out_shape=jax.ShapeDtypeStruct((R, n_cols), out_dtype),
        compiler_params=_params(
            ("arbitrary", "arbitrary"),
            _dense_vmem(tile, K, tn, n_w, ws[0].dtype.itemsize,
                        jnp.dtype(out_dtype).itemsize, 2)),
        name=name,
    )(tile_expert, n_tiles, tile_valid, a, *ws)


def _combine_kernel(dest_ref, ys_hbm, wts_ref, x_ref, g_ref, gate_ref, o_ref,
                    stage, sems, *, tile, tiles_per_seq, n_steps):
    t = pl.program_id(0) * tiles_per_seq + pl.program_id(1)
    slot = t % 2

    def issue_tile(tt, to_slot):
        n0 = tt * tile

        def issue(rr, carry):
            a = (n0 + rr) * TOP_K
            for k in range(TOP_K):
                pltpu.make_async_copy(ys_hbm.at[pl.ds(dest_ref[a + k], 1), :],
                                      stage.at[to_slot, k, pl.ds(rr, 1), :],
                                      sems.at[to_slot, k]).start()
            return carry
        lax.fori_loop(0, tile, issue, 0, unroll=4)

    @pl.when(t == 0)
    def _first_tile():
        issue_tile(0, 0)

    @pl.when(t + 1 < n_steps)
    def _prefetch_next():
        issue_tile(t + 1, 1 - slot)

    for k in range(TOP_K):
        pltpu.make_async_copy(ys_hbm.at[pl.ds(0, tile), :], stage.at[slot, k],
                              sems.at[slot, k]).wait()
    w = wts_ref[...]
    y = w[:, 0:1] * stage[slot, 0] + w[:, 1:2] * stage[slot, 1]
    o_ref[...] = x_ref[...] + gate_ref[...] * _rms(y, g_ref[...])


def _combine(dest_flat, ys, wts3, x, gains, mod, l, g_idx, gate_idx):
    B, S, D = x.shape
    tile = min(S, 256)
    row = pl.BlockSpec((None, tile, D), lambda b, i, dest: (b, i, 0))
    grid_spec = pltpu.PrefetchScalarGridSpec(
        num_scalar_prefetch=1,
        grid=(B, S // tile),
        in_specs=[pl.BlockSpec(memory_space=pl.ANY),
                  pl.BlockSpec((None, tile, LANES), lambda b, i, dest: (b, i, 0)),
                  row,
                  pl.BlockSpec((None, None, 1, D), lambda b, i, dest: (l, g_idx, 0, 0)),
                  pl.BlockSpec((None, None, None, 1, D),
                               lambda b, i, dest: (l, b, gate_idx, 0, 0))],
        out_specs=row,
        scratch_shapes=[pltpu.VMEM((2, TOP_K, tile, D), F32),
                        pltpu.SemaphoreType.DMA((2, TOP_K))],
    )
    return pl.pallas_call(
        functools.partial(_combine_kernel, tile=tile, tiles_per_seq=S // tile,
                          n_steps=B * (S // tile)),
        grid_spec=grid_spec,
        out_shape=jax.ShapeDtypeStruct((B, S, D), F32),
        compiler_params=_params(("arbitrary", "arbitrary"), 12 * tile * D * 4),
        name="moe_combine",
    )(dest_flat, ys, wts3, x, gains, mod)


def kernel(x, c, norm_gains, w_mod, b_mod, w_in, b_forget, w_pool, pool_scale, w_out,
           w_ffn_gate, w_ffn_up, w_ffn_down, w_router, w_exp_gate, w_exp_up, w_exp_down):
    B, S, D = x.shape
    L = w_mod.shape[0]
    N = B * S
    H = b_forget.shape[1]
    A = H * HEAD_DIM
    P = pool_scale.shape[1]
    assert A + P == D and w_in.shape[2] == 3 * A + H + P and H <= LANES
    assert P % LANES == 0 and A % LANES == 0
    E = w_router.shape[2]
    tm = min(N, 1024)
    tm_wide = min(N, 2048)
    tn_fu = 2 * LANES
    fu_cols = -(-(P + LANES) // tn_fu) * tn_fu

    mod = _mod(c, w_mod, b_mod)
    gains = norm_gains.reshape(L, 4, 1, D)
    h = _norm_mod(x, gains, mod, 0, 0, 0, 1, BF16)

    qk_scale = jnp.concatenate([jnp.full((1, A), HEAD_DIM ** -0.5 * LOG2_E, F32),
                                jnp.ones((1, 2 * A), F32)], axis=1)
    pool_scale3 = pool_scale.reshape(L, 1, P)
    w_in_t = jnp.swapaxes(w_in, 1, 2)

    for l in range(L):
        h2 = h.reshape(N, D)
        qkv_h = _qkv_proj(h2, w_in_t, l, 3 * A, tm_wide, min(3 * A, 512), qk_scale)
        w_fu_t = jnp.concatenate([w_in_t[l, 3 * A + H:], w_in_t[l, 3 * A:3 * A + H],
                                  jnp.zeros((fu_cols - P - H, D), F32)], axis=0)
        fu = _matmul_nt(h2, w_fu_t, tm, tn_fu, F32, "in_proj_gate_pool")
        fu3 = fu.reshape(B, S, fu_cols)
        b_pad = jnp.pad(b_forget[l], (0, LANES - H)).reshape(1, LANES)
        f_sh, f_hs = _fox_prep(fu3, b_pad, P // LANES)
        attn = _attention(qkv_h.reshape(3 * H, B, S, HEAD_DIM), f_sh,
                          f_hs.reshape(B, LANES, 1, S), H)
        pooled = _pool(fu3, w_pool, pool_scale3, l)
        mixed = _matmul_two(attn.reshape(N, A), pooled.reshape(N, P), w_out, (l,), tm_wide,
                            min(D, 512), F32, "out_proj")

        i = l // 2
        if l % 2 == 0:
            x, h = _resid_norm(x, mixed.reshape(B, S, D), gains, mod, l, 1, 2,
                               nxt=(l, 2, 3, 4))
            FF = w_ffn_gate.shape[2]
            tn_ff = 256 if FF % 256 == 0 else LANES
            act = _gateup(h.reshape(N, D), w_ffn_gate, w_ffn_up, (i,), tm_wide, tn_ff)
            y = _down(act, w_ffn_down, (i,), tm, min(D, 256))
            y3 = y.reshape(B, S, D)
            if l + 1 < L:
                x, h = _resid_norm(x, y3, gains, mod, l, 3, 5, nxt=(l + 1, 0, 0, 1))
            else:
                (x,) = _resid_norm(x, y3, gains, mod, l, 3, 5)
        else:
            wr_pad = jnp.pad(w_router[i], ((0, 0), (0, LANES - E)))
            x, hp, logits = _resid_norm(x, mixed.reshape(B, S, D), gains, mod, l, 1, 2,
                                        nxt=(l, 2, 3, 4), h_dtype=U32, w_router=wr_pad)
            tile = min(N, 512)
            max_tiles = (N * TOP_K) // tile + E
            assert max_tiles <= LANES
            dest, wts, meta = _route(logits.reshape(N, LANES), E, tile)
            dest_flat = dest[:, :TOP_K].reshape(N * TOP_K)
            tile_expert = meta[0, :max_tiles]
            n_tiles = meta[1, :1]
            tile_valid = meta[2, :max_tiles]
            xs = _gather(dest_flat, n_tiles, hp.reshape(N, D // 2), tile, max_tiles)
            tn_e = min(w_exp_gate.shape[3], 512)
            act = _grouped(_gateup_kernel, xs, (w_exp_gate, w_exp_up), i,
                           tile_expert, n_tiles, tile_valid, tile, tn_e, BF16, "moe_gate_up")
            ys = _grouped(_mm_kernel, act, (w_exp_down,), i,
                          tile_expert, n_tiles, tile_valid, tile, min(D, 1024), F32, "moe_down")
            x = _combine(dest_flat, ys, wts.reshape(B, S, LANES), x, gains, mod, l, 3, 5)
            if l + 1 < L:
                h = _norm_mod(x, gains, mod, l + 1, 0, 0, 1, BF16)
    return x
```

```python
import functools

import jax
import jax.numpy as jnp
from jax import lax
from jax.experimental import pallas as pl
from jax.experimental.pallas import tpu as pltpu

F32 = jnp.float32
BF16 = jnp.bfloat16
I32 = jnp.int32

HEAD_DIM = 128
POOL_WINDOWS = (2, 4, 8, 16)
N_MOD = 6
TOP_K = 2
RMS_EPS = 1e-6
LOG2_E = 1.4426950408889634
MOE_ROW_STEP = 128

LANES = 128
SUBLANES = 8
V7X_VMEM_BYTES = 64 * 2**20
VMEM_CAP_BYTES = V7X_VMEM_BYTES - 6 * 2**20


def _params(semantics, vmem_bytes):
    limit = int(min(max(vmem_bytes * 5 // 4, 16 * 2**20), VMEM_CAP_BYTES))
    return pltpu.CompilerParams(dimension_semantics=semantics, vmem_limit_bytes=limit)


def _rms(x, gain):
    inv = lax.rsqrt(jnp.mean(x * x, axis=-1, keepdims=True) + RMS_EPS)
    return x * inv * gain


def _mod_kernel(c_ref, w_ref, b_ref, o_ref):
    c = c_ref[...]
    c_act = (c * jax.nn.sigmoid(c)).astype(BF16)
    o_ref[...] = jnp.dot(c_act, w_ref[...].astype(BF16),
                         preferred_element_type=F32) + b_ref[...]


def _mod(c, w_mod, b_mod):
    B, D = c.shape
    L, _, NM = w_mod.shape
    rows = -(-B // SUBLANES) * SUBLANES
    c_pad = jnp.pad(c, ((0, rows - B), (0, 0)))
    tn = 512
    out = pl.pallas_call(
        _mod_kernel,
        grid=(L, NM // tn),
        in_specs=[
            pl.BlockSpec((rows, D), lambda l, j: (0, 0)),
            pl.BlockSpec((None, D, tn), lambda l, j: (l, 0, j)),
            pl.BlockSpec((None, 1, tn), lambda l, j: (l, 0, j)),
        ],
        out_specs=pl.BlockSpec((None, rows, tn), lambda l, j: (l, 0, j)),
        out_shape=jax.ShapeDtypeStruct((L, rows, NM), F32),
        compiler_params=_params(("arbitrary", "arbitrary"), 3 * D * tn * 4),
        name="adaln_mod",
    )(c_pad, w_mod, b_mod.reshape(L, 1, NM))
    return out[:, :B].reshape(L, B, N_MOD, 1, D)


def _mod_spec(l, idx, D):
    return pl.BlockSpec((None, None, None, 1, D), lambda b, i: (l, b, idx, 0, 0))


def _gain_spec(l, idx, D):
    return pl.BlockSpec((None, None, 1, D), lambda b, i: (l, idx, 0, 0))


def _norm_mod_kernel(x_ref, g_ref, sc_ref, sh_ref, o_ref):
    h = _rms(x_ref[...], g_ref[...])
    o_ref[...] = (h * (1.0 + sc_ref[...]) + sh_ref[...]).astype(o_ref.dtype)


def _norm_mod(x, gains, mod, l, gain_idx, shift_idx, scale_idx, out_dtype):
    B, S, D = x.shape
    ts = min(S, 256)
    row = pl.BlockSpec((None, ts, D), lambda b, i: (b, i, 0))
    return pl.pallas_call(
        _norm_mod_kernel,
        grid=(B, S // ts),
        in_specs=[row, _gain_spec(l, gain_idx, D), _mod_spec(l, scale_idx, D),
                  _mod_spec(l, shift_idx, D)],
        out_specs=row,
        out_shape=jax.ShapeDtypeStruct((B, S, D), out_dtype),
        compiler_params=_params(("arbitrary", "arbitrary"), 4 * ts * D * 4),
        name="pre_norm",
    )(x, gains, mod, mod)


def _split_bf16(v):
    hi = v.astype(BF16)
    lo = (v - hi.astype(F32)).astype(BF16)
    return hi, lo


def _resid_norm_kernel(*refs, emit_h, emit_logits):
    x_ref, y_ref, ga_ref, gate_ref = refs[:4]
    pos = 4
    if emit_h:
        gb_ref, sc_ref, sh_ref = refs[pos:pos + 3]
        pos += 3
    if emit_logits:
        wr_ref = refs[pos]
        pos += 1
    xo_ref = refs[pos]
    pos += 1
    x_new = x_ref[...] + gate_ref[...] * _rms(y_ref[...], ga_ref[...])
    xo_ref[...] = x_new
    if emit_h:
        h_ref = refs[pos]
        pos += 1
        h = _rms(x_new, gb_ref[...]) * (1.0 + sc_ref[...]) + sh_ref[...]
        h_ref[...] = h.astype(h_ref.dtype)
        if emit_logits:
            lg_ref = refs[pos]
            h_hi, h_lo = _split_bf16(h)
            w_hi, w_lo = _split_bf16(wr_ref[...])
            lg_ref[...] = (jnp.dot(h_hi, w_hi, preferred_element_type=F32)
                           + jnp.dot(h_lo, w_hi, preferred_element_type=F32)
                           + jnp.dot(h_hi, w_lo, preferred_element_type=F32))


def _resid_norm(x, y, gains, mod, la, ga_idx, gate_idx, nxt=None, h_dtype=BF16,
                w_router=None):
    B, S, D = x.shape
    ts = min(S, 256)
    row = pl.BlockSpec((None, ts, D), lambda b, i: (b, i, 0))
    in_specs = [row, row, _gain_spec(la, ga_idx, D), _mod_spec(la, gate_idx, D)]
    args = [x, y, gains, mod]
    out_specs = [row]
    out_shape = [jax.ShapeDtypeStruct((B, S, D), F32)]
    emit_h = nxt is not None
    emit_logits = w_router is not None
    if emit_h:
        lb, gb_idx, shift_idx, scale_idx = nxt
        in_specs += [_gain_spec(lb, gb_idx, D), _mod_spec(lb, scale_idx, D),
                     _mod_spec(lb, shift_idx, D)]
        args += [gains, mod, mod]
        out_specs.append(row)
        out_shape.append(jax.ShapeDtypeStruct((B, S, D), h_dtype))
    if emit_logits:
        in_specs.append(pl.BlockSpec((D, LANES), lambda b, i: (0, 0)))
        args.append(w_router)
        out_specs.append(pl.BlockSpec((None, ts, LANES), lambda b, i: (b, i, 0)))
        out_shape.append(jax.ShapeDtypeStruct((B, S, LANES), F32))
    outs = pl.pallas_call(
        functools.partial(_resid_norm_kernel, emit_h=emit_h, emit_logits=emit_logits),
        grid=(B, S // ts),
        in_specs=in_specs,
        out_specs=out_specs,
        out_shape=out_shape,
        compiler_params=_params(("arbitrary", "arbitrary"), 10 * ts * D * 4),
        name="resid_norm",
    )(*args)
    return outs


def _mm_kernel(a_ref, w_ref, o_ref):
    o_ref[...] = jnp.dot(a_ref[...], w_ref[...].astype(BF16),
                         preferred_element_type=F32).astype(o_ref.dtype)


def _nt_dot(a, wt):
    return lax.dot_general(a, wt, (((1,), (1,)), ((), ())), preferred_element_type=F32)


def _mm_nt_kernel(a_ref, wt_ref, o_ref):
    o_ref[...] = _nt_dot(a_ref[...], wt_ref[...].astype(BF16)).astype(o_ref.dtype)


def _qkv_kernel(a_ref, wt_ref, cs_ref, o_ref):
    acc = _nt_dot(a_ref[...], wt_ref[...].astype(BF16)) * cs_ref[...]
    for hh in range(o_ref.shape[0]):
        o_ref[hh] = acc[:, hh * HEAD_DIM:(hh + 1) * HEAD_DIM].astype(o_ref.dtype)


def _mm_two_kernel(a1_ref, a2_ref, w_ref, o_ref):
    k1 = a1_ref.shape[1]
    w = w_ref[...].astype(BF16)
    o_ref[...] = (jnp.dot(a1_ref[...], w[:k1], preferred_element_type=F32)
                  + jnp.dot(a2_ref[...], w[k1:], preferred_element_type=F32)).astype(o_ref.dtype)


def _gateup_kernel(a_ref, wg_ref, wu_ref, o_ref):
    a = a_ref[...]
    g = jnp.dot(a, wg_ref[...].astype(BF16), preferred_element_type=F32)
    u = jnp.dot(a, wu_ref[...].astype(BF16), preferred_element_type=F32)
    o_ref[...] = (g * jax.nn.sigmoid(g) * u).astype(o_ref.dtype)


def _w_spec(w, tn, lead):
    K = w.shape[-2]
    nlead = len(lead)
    return pl.BlockSpec((None,) * nlead + (K, tn), lambda i, j: tuple(lead) + (0, j))


def _a_spec(tm, K):
    return pl.BlockSpec((tm, K), lambda i, j: (i, 0), pipeline_mode=pl.Buffered(1))


def _dense_vmem(tm, K, tn, n_w, w_bytes, out_bytes, a_buffers):
    a = a_buffers * tm * K * 2
    w = n_w * (2 * K * tn * w_bytes + K * tn * 2)
    o = 2 * tm * tn * out_bytes + n_w * tm * tn * 4
    return a + w + o


def _qkv_proj(a, wt, l, n_cols, tm, tn, colscale):
    M, K = a.shape
    hpb = tn // HEAD_DIM
    return pl.pallas_call(
        _qkv_kernel,
        grid=(M // tm, n_cols // tn),
        in_specs=[_a_spec(tm, K),
                  pl.BlockSpec((None, tn, K), lambda i, j: (l, j, 0)),
                  pl.BlockSpec((1, tn), lambda i, j: (0, j))],
        out_specs=pl.BlockSpec((hpb, tm, HEAD_DIM), lambda i, j: (j, i, 0)),
        out_shape=jax.ShapeDtypeStruct((n_cols // HEAD_DIM, M, HEAD_DIM), BF16),
        compiler_params=_params(("arbitrary", "arbitrary"),
                                _dense_vmem(tm, K, tn, 1, wt.dtype.itemsize, 2, 1)),
        name="in_proj_qkv",
    )(a, wt, colscale)


def _matmul_nt(a, wt, tm, tn, out_dtype, name):
    M, K = a.shape
    n_cols = wt.shape[0]
    return pl.pallas_call(
        _mm_nt_kernel,
        grid=(M // tm, n_cols // tn),
        in_specs=[pl.BlockSpec((tm, K), lambda i, j: (i, 0)),
                  pl.BlockSpec((tn, K), lambda i, j: (j, 0))],
        out_specs=pl.BlockSpec((tm, tn), lambda i, j: (i, j)),
        out_shape=jax.ShapeDtypeStruct((M, n_cols), out_dtype),
        compiler_params=_params(
            ("arbitrary", "arbitrary"),
            _dense_vmem(tm, K, tn, 1, wt.dtype.itemsize, jnp.dtype(out_dtype).itemsize, 2)),
        name=name,
    )(a, wt)


def _matmul_two(a1, a2, w, lead, tm, tn, out_dtype, name):
    M, K1 = a1.shape
    K2 = a2.shape[1]
    n_cols = w.shape[-1]
    return pl.pallas_call(
        _mm_two_kernel,
        grid=(M // tm, n_cols // tn),
        in_specs=[_a_spec(tm, K1), _a_spec(tm, K2), _w_spec(w, tn, lead)],
        out_specs=pl.BlockSpec((tm, tn), lambda i, j: (i, j)),
        out_shape=jax.ShapeDtypeStruct((M, n_cols), out_dtype),
        compiler_params=_params(
            ("arbitrary", "arbitrary"),
            _dense_vmem(tm, K1 + K2, tn, 1, w.dtype.itemsize, jnp.dtype(out_dtype).itemsize, 1)),
        name=name,
    )(a1, a2, w)


def _gateup(a, wg, wu, lead, tm, tn, name="ffn_gate_up"):
    M, K = a.shape
    n_cols = wg.shape[-1]
    return pl.pallas_call(
        _gateup_kernel,
        grid=(M // tm, n_cols // tn),
        in_specs=[_a_spec(tm, K), _w_spec(wg, tn, lead), _w_spec(wu, tn, lead)],
        out_specs=pl.BlockSpec((tm, tn), lambda i, j: (i, j)),
        out_shape=jax.ShapeDtypeStruct((M, n_cols), BF16),
        compiler_params=_params(("arbitrary", "arbitrary"),
                                _dense_vmem(tm, K, tn, 2, wg.dtype.itemsize, 2, 1)),
        name=name,
    )(a, wg, wu)


def _down(a, w, lead, tm, tn, name="ffn_down"):
    M, K = a.shape
    n_cols = w.shape[-1]
    return pl.pallas_call(
        _mm_kernel,
        grid=(M // tm, n_cols // tn),
        in_specs=[_a_spec(tm, K), _w_spec(w, tn, lead)],
        out_specs=pl.BlockSpec((tm, tn), lambda i, j: (i, j)),
        out_shape=jax.ShapeDtypeStruct((M, n_cols), F32),
        compiler_params=_params(("arbitrary", "arbitrary"),
                                _dense_vmem(tm, K, tn, 1, w.dtype.itemsize, 4, 1)),
        name=name,
    )(a, w)


def _fox_prep_kernel(f_ref, b_ref, fsh_ref, fhs_ref):
    z = f_ref[...] + b_ref[...]
    acc = jnp.minimum(z, 0.0) - jnp.log1p(jnp.exp(-jnp.abs(z)))
    S = acc.shape[0]
    row = lax.broadcasted_iota(I32, acc.shape, 0)
    d = 1
    while d < S:
        acc = acc + jnp.where(row >= d, pltpu.roll(acc, d, 0), 0.0)
        d *= 2
    acc = acc * LOG2_E
    fsh_ref[...] = acc
    fhs_ref[...] = acc.T


def _fox_prep(fu3, b_pad, col_block):
    B, S, _ = fu3.shape
    return pl.pallas_call(
        _fox_prep_kernel,
        grid=(B,),
        in_specs=[pl.BlockSpec((None, S, LANES), lambda b: (b, 0, col_block)),
                  pl.BlockSpec((1, LANES), lambda b: (0, 0))],
        out_specs=[pl.BlockSpec((None, S, LANES), lambda b: (b, 0, 0)),
                   pl.BlockSpec((None, LANES, S), lambda b: (b, 0, 0))],
        out_shape=[jax.ShapeDtypeStruct((B, S, LANES), F32),
                   jax.ShapeDtypeStruct((B, LANES, S), F32)],
        compiler_params=_params(("arbitrary",), 12 * S * LANES * 4),
        name="fox_prep",
    )(fu3, b_pad)


def _attn_kernel(q_ref, k_ref, v_ref, fq_ref, fk_ref, o_ref, fq_s, *, blk):
    heads = q_ref.shape[0]
    for hh in range(heads):
        _attn_head(pl.program_id(1) * heads + hh, q_ref.at[hh], k_ref.at[hh], v_ref.at[hh],
                   fq_ref, fk_ref.at[hh], o_ref, hh * HEAD_DIM, fq_s.at[hh], blk)


def _attn_head(h, q_ref, k_ref, v_ref, fq_ref, fk_ref, o_ref, o_col, fq_s, blk):
    S = q_ref.shape[0]
    n_blk = S // blk
    lane = lax.broadcasted_iota(I32, fq_ref.shape, 1)
    fq_col = jnp.sum(jnp.where(lane == h, fq_ref[...], 0.0), axis=-1, keepdims=True)
    fq_s[...] = jnp.broadcast_to(fq_col, fq_s.shape)

    def logits(row_lo, n_rows, key_lo, n_keys):
        q = q_ref[row_lo:row_lo + n_rows, :]
        fq = fq_s[row_lo:row_lo + n_rows, :]
        s = lax.dot_general(q, k_ref[key_lo:key_lo + n_keys, :], (((1,), (1,)), ((), ())),
                            preferred_element_type=F32)
        chunks = []
        for c in range(n_keys // LANES):
            lo = key_lo + c * LANES
            sc = s[:, c * LANES:(c + 1) * LANES] + (fq - fk_ref[:, lo:lo + LANES])
            if lo + LANES - 1 > row_lo:
                r = lax.broadcasted_iota(I32, (n_rows, LANES), 0) + row_lo
                k = lax.broadcasted_iota(I32, (n_rows, LANES), 1) + lo
                sc = jnp.where(k <= r, sc, -jnp.inf)
            chunks.append(sc)
        return chunks

    def online_update(state, chunks, v):
        m_row = jnp.max(functools.reduce(jnp.maximum, chunks), axis=-1, keepdims=True)
        m_new = m_row if state is None else jnp.maximum(state[0], m_row)
        m_new = jnp.broadcast_to(m_new, chunks[0].shape)
        ps = [jnp.exp2(sc - m_new) for sc in chunks]
        p_sum = functools.reduce(lambda a, b: a + b, ps)
        pv = jnp.dot(jnp.concatenate(ps, axis=1).astype(BF16), v, preferred_element_type=F32)
        if state is None:
            return m_new, p_sum, pv
        alpha = jnp.exp2(state[0] - m_new)
        return m_new, alpha * state[1] + p_sum, alpha * state[2] + pv

    for i in range(n_blk):
        r0 = i * blk
        state = None
        for j in range(i):
            k0 = j * blk
            state = online_update(state, logits(r0, blk, k0, blk), v_ref[k0:k0 + blk, :])
        _, l_b, acc = online_update(state, logits(r0, blk, r0, blk), v_ref[r0:r0 + blk, :])
        l_row = jnp.sum(l_b, axis=-1, keepdims=True)
        o_ref[r0:r0 + blk, o_col:o_col + HEAD_DIM] = (acc / l_row).astype(o_ref.dtype)


def _attention(qkv_h, f_sh, f_hs4, n_heads):
    _, B, S, _ = qkv_h.shape
    H = n_heads
    blk = min(S, 512)
    hps = 2 if H % 2 == 0 else 1
    head = lambda off: pl.BlockSpec((hps, None, S, HEAD_DIM),
                                    lambda b, g: (off // hps + g, b, 0, 0))
    return pl.pallas_call(
        functools.partial(_attn_kernel, blk=blk),
        grid=(B, H // hps),
        in_specs=[head(0), head(H), head(2 * H),
                  pl.BlockSpec((None, S, LANES), lambda b, g: (b, 0, 0)),
                  pl.BlockSpec((None, hps, 1, S), lambda b, g: (b, g, 0, 0))],
        out_specs=pl.BlockSpec((None, S, hps * HEAD_DIM), lambda b, g: (b, 0, g)),
        out_shape=jax.ShapeDtypeStruct((B, S, H * HEAD_DIM), BF16),
        scratch_shapes=[pltpu.VMEM((hps, S, LANES), F32)],
        compiler_params=_params(
            ("arbitrary", "arbitrary"),
            hps * (8 * S * HEAD_DIM * 2 + 3 * S * LANES * 4 + 8 * blk * blk * 4)),
        name="fox_attention",
    )(qkv_h, qkv_h, qkv_h, f_sh, f_hs4)


def _pool_kernel(u_ref, w_ref, sc_ref, o_ref):
    g = pl.program_id(1)
    u = u_ref[...]
    row = lax.broadcasted_iota(I32, u.shape, 0)

    def shifted(v, d):
        return jnp.where(row >= d, pltpu.roll(v, d, 0), 0.0)

    s2 = u + shifted(u, 1)
    s4 = s2 + shifted(s2, 2)
    s8 = s4 + shifted(s4, 4)
    s16 = s8 + shifted(s8, 8)
    win = jnp.where(g == 0, s2, jnp.where(g == 1, s4, jnp.where(g == 2, s8, s16)))
    width = lax.shift_left(jnp.int32(POOL_WINDOWS[0]), g)
    count = jnp.minimum(row + 1, width).astype(F32)
    pooled = win / count - u
    y = jnp.dot(pooled.astype(BF16), w_ref[...].astype(BF16), preferred_element_type=F32)
    o_ref[...] = (y * sc_ref[...]).astype(o_ref.dtype)


def _pool(fu3, w_pool, pool_scale3, l):
    B, S, _ = fu3.shape
    G, C = w_pool.shape[1], w_pool.shape[2]
    assert G == len(POOL_WINDOWS) and all(
        w == POOL_WINDOWS[0] << k for k, w in enumerate(POOL_WINDOWS))
    return pl.pallas_call(
        _pool_kernel,
        grid=(B, G),
        in_specs=[pl.BlockSpec((None, S, C), lambda b, g: (b, 0, g)),
                  pl.BlockSpec((None, None, C, C), lambda b, g: (l, g, 0, 0)),
                  pl.BlockSpec((None, 1, C), lambda b, g: (l, 0, g))],
        out_specs=pl.BlockSpec((None, S, C), lambda b, g: (b, 0, g)),
        out_shape=jax.ShapeDtypeStruct((B, S, G * C), BF16),
        compiler_params=_params(("arbitrary", "arbitrary"), 14 * S * C * 4),
        name="pool_mixer",
    )(fu3, w_pool, pool_scale3)


def _route_kernel(lg_ref, dest_ref, wts_ref, meta_ref, m_all, m_top, tri, *,
                  n_experts, tile, chunk):
    N = lg_ref.shape[0]
    n_chunks = N // chunk
    lane = lax.broadcasted_iota(I32, (chunk, LANES), 1).astype(F32)
    lane1 = lax.broadcasted_iota(I32, (1, LANES), 1).astype(F32)
    r = lax.broadcasted_iota(I32, (chunk, chunk), 0)
    c = lax.broadcasted_iota(I32, (chunk, chunk), 1)
    tri[...] = jnp.where(c < r, 1.0, 0.0).astype(BF16)

    def select(ci, counts):
        rows = pl.ds(pl.multiple_of(ci * chunk, chunk), chunk)
        lg = jnp.where(lane < n_experts, lg_ref[rows, :], -jnp.inf)
        v1 = jnp.max(lg, axis=-1, keepdims=True)
        i1 = jnp.min(jnp.where(lg == v1, lane, float(LANES)), axis=-1, keepdims=True)
        first = lane == i1
        lg2 = jnp.where(first, -jnp.inf, lg)
        v2 = jnp.max(lg2, axis=-1, keepdims=True)
        i2 = jnp.min(jnp.where(lg2 == v2, lane, float(LANES)), axis=-1, keepdims=True)
        second = lane == i2
        e2 = jnp.exp(v2 - v1)
        w1 = 1.0 / (1.0 + e2)
        w2 = e2 / (1.0 + e2)
        wts_ref[rows, :] = jnp.where(lane == 0.0, w1, jnp.where(lane == 1.0, w2, 0.0))
        both = jnp.where(first | second, 1.0, 0.0)
        m_all[rows, :] = both.astype(BF16)
        m_top[rows, :] = jnp.where(first, 1.0, 0.0).astype(BF16)
        return counts + jnp.sum(both, axis=0, keepdims=True)

    counts = lax.fori_loop(0, n_chunks, select, jnp.zeros((1, LANES), F32))
    padded = jnp.ceil(counts / tile) * tile
    start = jnp.zeros((1, LANES), F32)
    tile_row = lane1 * tile
    tile_expert = jnp.zeros((1, LANES), F32)
    last_used = jnp.zeros((1, 1), F32)
    running = jnp.zeros((1, 1), F32)
    tile_valid = jnp.zeros((1, LANES), F32)
    for e in range(n_experts):
        rows_e = jnp.sum(jnp.where(lane1 == e, padded, 0.0), axis=-1, keepdims=True)
        count_e = jnp.sum(jnp.where(lane1 == e, counts, 0.0), axis=-1, keepdims=True)
        in_group = (tile_row >= running) & (tile_row < running + rows_e)
        tile_valid = jnp.where(in_group, jnp.minimum(running + count_e - tile_row, float(tile)),
                               tile_valid)
        start = start + jnp.where(lane1 > e, rows_e, 0.0)
        running = running + rows_e
        tile_expert = tile_expert + jnp.where(tile_row >= running, 1.0, 0.0)
        last_used = jnp.where(rows_e > 0.0, float(e), last_used)
    tile_expert = jnp.minimum(tile_expert, last_used)
    n_tiles = running / tile
    sub = lax.broadcasted_iota(I32, (SUBLANES, LANES), 0)
    meta_ref[...] = jnp.where(sub == 0, tile_expert,
                              jnp.where(sub == 1, n_tiles,
                                        jnp.where(sub == 2, tile_valid, 0.0))).astype(I32)

    def place(ci, carry):
        rows = pl.ds(pl.multiple_of(ci * chunk, chunk), chunk)
        both = m_all[rows, :]
        top = m_top[rows, :].astype(F32)
        rank = jnp.dot(tri[...], both, preferred_element_type=F32)
        pos = start + carry + rank
        both_f = both.astype(F32)
        d1 = jnp.sum(top * pos, axis=-1, keepdims=True)
        d2 = jnp.sum((both_f - top) * pos, axis=-1, keepdims=True)
        dest_ref[rows, :] = jnp.where(lane == 0.0, d1, jnp.where(lane == 1.0, d2, 0.0)).astype(I32)
        return carry + jnp.sum(both_f, axis=0, keepdims=True)

    lax.fori_loop(0, n_chunks, place, jnp.zeros((1, LANES), F32))


def _route(logits, n_experts, tile):
    N = logits.shape[0]
    chunk = min(N, 512)
    return pl.pallas_call(
        functools.partial(_route_kernel, n_experts=n_experts, tile=tile, chunk=chunk),
        out_shape=[jax.ShapeDtypeStruct((N, LANES), I32),
                   jax.ShapeDtypeStruct((N, LANES), F32),
                   jax.ShapeDtypeStruct((SUBLANES, LANES), I32)],
        scratch_shapes=[pltpu.VMEM((N, LANES), BF16), pltpu.VMEM((N, LANES), BF16),
                        pltpu.VMEM((chunk, chunk), BF16)],
        compiler_params=pltpu.CompilerParams(
            vmem_limit_bytes=int(min(max(N * LANES * 4 * 8, 16 * 2**20), VMEM_CAP_BYTES))),
        name="moe_route",
    )(logits)


def _gather_kernel(dest_ref, nt_ref, h_hbm, o_ref, src, stage, sems, *, tile, n_rows, n_assign):
    t = pl.program_id(0)
    n_used = nt_ref[0]
    slot = t % 2

    def issue_tile(tt, to_slot):
        base = tt * tile

        def issue(rr, carry):
            tok = src[base + rr]
            pltpu.make_async_copy(h_hbm.at[pl.ds(tok, 1), :],
                                  stage.at[to_slot, pl.ds(rr, 1), :], sems.at[to_slot]).start()
            return carry
        lax.fori_loop(0, tile, issue, 0, unroll=8)

    @pl.when(t == 0)
    def _invert_dest():
        def clear(rr, carry):
            src[rr] = 0
            return carry
        lax.fori_loop(0, n_rows, clear, 0, unroll=8)

        def scatter(n, carry):
            for k in range(TOP_K):
                src[dest_ref[n * TOP_K + k]] = n
            return carry
        lax.fori_loop(0, n_assign // TOP_K, scatter, 0, unroll=4)
        issue_tile(0, 0)

    @pl.when(t + 1 < n_used)
    def _prefetch_next():
        issue_tile(t + 1, 1 - slot)

    @pl.when(t < n_used)
    def _gather():
        pltpu.make_async_copy(h_hbm.at[pl.ds(0, tile), :], stage.at[slot], sems.at[slot]).wait()
        o_ref[...] = stage[slot].astype(o_ref.dtype)

    @pl.when(t >= n_used)
    def _unused_tile():
        o_ref[...] = jnp.zeros(o_ref.shape, o_ref.dtype)


def _gather(dest_flat, n_tiles, h2, tile, max_tiles):
    N, D = h2.shape
    n_rows = max_tiles * tile
    grid_spec = pltpu.PrefetchScalarGridSpec(
        num_scalar_prefetch=2,
        grid=(max_tiles,),
        in_specs=[pl.BlockSpec(memory_space=pl.ANY)],
        out_specs=pl.BlockSpec((tile, D), lambda t, dest, nt: (t, 0)),
        scratch_shapes=[pltpu.SMEM((n_rows,), I32), pltpu.VMEM((2, tile, D), F32),
                        pltpu.SemaphoreType.DMA((2,))],
    )
    return pl.pallas_call(
        functools.partial(_gather_kernel, tile=tile, n_rows=n_rows, n_assign=dest_flat.shape[0]),
        grid_spec=grid_spec,
        out_shape=jax.ShapeDtypeStruct((n_rows, D), BF16),
        compiler_params=_params(("arbitrary",), 4 * tile * D * 4),
        name="moe_gather",
    )(dest_flat, n_tiles, h2)


def _grouped_kernel(te_ref, nt_ref, nv_ref, a_ref, *rest, n_w, layer, tn, max_tiles, compute):
    w_hbm = rest[:n_w]
    o_ref = rest[n_w]
    w_buf = rest[n_w + 1:2 * n_w + 1]
    sems, state = rest[2 * n_w + 1:]
    j = pl.program_id(0)
    t = pl.program_id(1)
    n_used = nt_ref[0]

    def fetch(e, jj, slot):
        return [pltpu.make_async_copy(w_hbm[wi].at[layer, e, :, pl.ds(jj * tn, tn)],
                                      w_buf[wi].at[slot], sems.at[wi, slot])
                for wi in range(n_w)]

    @pl.when((j == 0) & (t == 0))
    def _first_fetch():
        state[0] = 0
        for cp in fetch(te_ref[0], 0, 0):
            cp.start()

    e = te_ref[t]
    run_start = (t < n_used) & ((t == 0) | (e != te_ref[jnp.maximum(t - 1, 0)]))

    @pl.when(run_start)
    def _swap_weights():
        slot = state[0]
        for cp in fetch(e, j, slot):
            cp.wait()
        t_next = lax.while_loop(
            lambda tt: (tt < n_used) & (te_ref[jnp.minimum(tt, max_tiles - 1)] == e),
            lambda tt: tt + 1, t + 1)
        same_sweep = t_next < n_used

        @pl.when(same_sweep)
        def _next_expert():
            for cp in fetch(te_ref[jnp.minimum(t_next, max_tiles - 1)], j, 1 - slot):
                cp.start()

        @pl.when(jnp.logical_not(same_sweep) & (j + 1 < pl.num_programs(0)))
        def _next_sweep():
            for cp in fetch(te_ref[0], j + 1, 1 - slot):
                cp.start()

        state[1] = slot
        state[0] = 1 - slot

    @pl.when(t < n_used)
    def _compute():
        slot = state[1]
        weights = [wb.at[slot] for wb in w_buf]
        n_valid = nv_ref[t]
        tile = a_ref.shape[0]
        for rows in range(MOE_ROW_STEP, tile + 1, MOE_ROW_STEP):
            @pl.when((n_valid > rows - MOE_ROW_STEP) & (n_valid <= rows))
            def _leading_rows(rows=rows):
                compute(a_ref.at[pl.ds(0, rows)], *weights, o_ref.at[pl.ds(0, rows)])
                if rows < tile:
                    o_ref[pl.ds(rows, tile - rows), :] = jnp.zeros((tile - rows, tn), o_ref.dtype)

    @pl.when(t >= n_used)
    def _unused_tile():
        o_ref[...] = jnp.zeros(o_ref.shape, o_ref.dtype)


def _grouped(compute, a, ws, l, tile_expert, n_tiles, tile_valid, tile, tn, out_dtype, name):
    R, K = a.shape
    n_cols = ws[0].shape[-1]
    max_tiles = R // tile
    n_w = len(ws)
    assert tile % MOE_ROW_STEP == 0
    grid_spec = pltpu.PrefetchScalarGridSpec(
        num_scalar_prefetch=3,
        grid=(n_cols // tn, max_tiles),
        in_specs=[pl.BlockSpec((tile, K),
                               lambda j, t, te, nt, nv: (jnp.minimum(t, nt[0] - 1), 0))]
        + [pl.BlockSpec(memory_space=pl.ANY)] * n_w,
        out_specs=pl.BlockSpec((tile, tn), lambda j, t, te, nt, nv: (t, j)),
        scratch_shapes=[pltpu.VMEM((2, K, tn), ws[0].dtype) for _ in range(n_w)]
        + [pltpu.SemaphoreType.DMA((n_w, 2)), pltpu.SMEM((2,), I32)],
    )
    return pl.pallas_call(
        functools.partial(_grouped_kernel, n_w=n_w, layer=l, tn=tn, max_tiles=max_tiles,
                          compute=compute),
        grid_spec=grid_spec,
        out_shape=jax.ShapeDtypeStruct((R, n_cols), out_dtype),
        compiler_params=_params(
            ("arbitrary", "arbitrary"),
            _dense_vmem(tile, K, tn, n_w, ws[0].dtype.itemsize,
                        jnp.dtype(out_dtype).itemsize, 2)),
        name=name,
    )(tile_expert, n_tiles, tile_valid, a, *ws)


def _combine_kernel(dest_ref, ys_hbm, wts_ref, x_ref, g_ref, gate_ref, o_ref,
                    stage, sems, *, tile, tiles_per_seq, n_steps):
    t = pl.program_id(0) * tiles_per_seq + pl.program_id(1)
    slot = t % 2

    def issue_tile(tt, to_slot):
        n0 = tt * tile

        def issue(rr, carry):
            a = (n0 + rr) * TOP_K
            for k in range(TOP_K):
                pltpu.make_async_copy(ys_hbm.at[pl.ds(dest_ref[a + k], 1), :],
                                      stage.at[to_slot, k, pl.ds(rr, 1), :],
                                      sems.at[to_slot, k]).start()
            return carry
        lax.fori_loop(0, tile, issue, 0, unroll=4)

    @pl.when(t == 0)
    def _first_tile():
        issue_tile(0, 0)

    @pl.when(t + 1 < n_steps)
    def _prefetch_next():
        issue_tile(t + 1, 1 - slot)

    for k in range(TOP_K):
        pltpu.make_async_copy(ys_hbm.at[pl.ds(0, tile), :], stage.at[slot, k],
                              sems.at[slot, k]).wait()
    w = wts_ref[...]
    y = w[:, 0:1] * stage[slot, 0] + w[:, 1:2] * stage[slot, 1]
    o_ref[...] = x_ref[...] + gate_ref[...] * _rms(y, g_ref[...])


def _combine(dest_flat, ys, wts3, x, gains, mod, l, g_idx, gate_idx):
    B, S, D = x.shape
    tile = min(S, 256)
    row = pl.BlockSpec((None, tile, D), lambda b, i, dest: (b, i, 0))
    grid_spec = pltpu.PrefetchScalarGridSpec(
        num_scalar_prefetch=1,
        grid=(B, S // tile),
        in_specs=[pl.BlockSpec(memory_space=pl.ANY),
                  pl.BlockSpec((None, tile, LANES), lambda b, i, dest: (b, i, 0)),
                  row,
                  pl.BlockSpec((None, None, 1, D), lambda b, i, dest: (l, g_idx, 0, 0)),
                  pl.BlockSpec((None, None, None, 1, D),
                               lambda b, i, dest: (l, b, gate_idx, 0, 0))],
        out_specs=row,
        scratch_shapes=[pltpu.VMEM((2, TOP_K, tile, D), F32),
                        pltpu.SemaphoreType.DMA((2, TOP_K))],
    )
    return pl.pallas_call(
        functools.partial(_combine_kernel, tile=tile, tiles_per_seq=S // tile,
                          n_steps=B * (S // tile)),
        grid_spec=grid_spec,
        out_shape=jax.ShapeDtypeStruct((B, S, D), F32),
        compiler_params=_params(("arbitrary", "arbitrary"), 12 * tile * D * 4),
        name="moe_combine",
    )(dest_flat, ys, wts3, x, gains, mod)


def kernel(x, c, norm_gains, w_mod, b_mod, w_in, b_forget, w_pool, pool_scale, w_out,
           w_ffn_gate, w_ffn_up, w_ffn_down, w_router, w_exp_gate, w_exp_up, w_exp_down):
    B, S, D = x.shape
    L = w_mod.shape[0]
    N = B * S
    H = b_forget.shape[1]
    A = H * HEAD_DIM
    P = pool_scale.shape[1]
    assert A + P == D and w_in.shape[2] == 3 * A + H + P and H <= LANES
    assert P % LANES == 0 and A % LANES == 0
    E = w_router.shape[2]
    tm = min(N, 1024)
    tm_wide = min(N, 2048)
    tn_fu = 2 * LANES
    fu_cols = -(-(P + LANES) // tn_fu) * tn_fu

    mod = _mod(c, w_mod, b_mod)
    gains = norm_gains.reshape(L, 4, 1, D)
    h = _norm_mod(x, gains, mod, 0, 0, 0, 1, BF16)

    qk_scale = jnp.concatenate([jnp.full((1, A), HEAD_DIM ** -0.5 * LOG2_E, F32),
                                jnp.ones((1, 2 * A), F32)], axis=1)
    pool_scale3 = pool_scale.reshape(L, 1, P)
    w_in_t = jnp.swapaxes(w_in, 1, 2)

    for l in range(L):
        h2 = h.reshape(N, D)
        qkv_h = _qkv_proj(h2, w_in_t, l, 3 * A, tm_wide, min(3 * A, 512), qk_scale)
        w_fu_t = jnp.concatenate([w_in_t[l, 3 * A + H:], w_in_t[l, 3 * A:3 * A + H],
                                  jnp.zeros((fu_cols - P - H, D), F32)], axis=0)
        fu = _matmul_nt(h2, w_fu_t, tm, tn_fu, F32, "in_proj_gate_pool")
        fu3 = fu.reshape(B, S, fu_cols)
        b_pad = jnp.pad(b_forget[l], (0, LANES - H)).reshape(1, LANES)
        f_sh, f_hs = _fox_prep(fu3, b_pad, P // LANES)
        attn = _attention(qkv_h.reshape(3 * H, B, S, HEAD_DIM), f_sh,
                          f_hs.reshape(B, LANES, 1, S), H)
        pooled = _pool(fu3, w_pool, pool_scale3, l)
        mixed = _matmul_two(attn.reshape(N, A), pooled.reshape(N, P), w_out, (l,), tm_wide,
                            min(D, 512), F32, "out_proj")

        i = l // 2
        if l % 2 == 0:
            x, h = _resid_norm(x, mixed.reshape(B, S, D), gains, mod, l, 1, 2,
                               nxt=(l, 2, 3, 4))
            FF = w_ffn_gate.shape[2]
            tn_ff = 256 if FF % 256 == 0 else LANES
            act = _gateup(h.reshape(N, D), w_ffn_gate, w_ffn_up, (i,), tm_wide, tn_ff)
            y = _down(act, w_ffn_down, (i,), tm, min(D, 256))
            y3 = y.reshape(B, S, D)
            if l + 1 < L:
                x, h = _resid_norm(x, y3, gains, mod, l, 3, 5, nxt=(l + 1, 0, 0, 1))
            else:
                (x,) = _resid_norm(x, y3, gains, mod, l, 3, 5)
        else:
            wr_pad = jnp.pad(w_router[i], ((0, 0), (0, LANES - E)))
            x, hf, logits = _resid_norm(x, mixed.reshape(B, S, D), gains, mod, l, 1, 2,
                                        nxt=(l, 2, 3, 4), h_dtype=F32, w_router=wr_pad)
            tile = min(N, 512)
            max_tiles = (N * TOP_K) // tile + E
            assert max_tiles <= LANES
            dest, wts, meta = _route(logits.reshape(N, LANES), E, tile)
            dest_flat = dest[:, :TOP_K].reshape(N * TOP_K)
            tile_expert = meta[0, :max_tiles]
            n_tiles = meta[1, :1]
            tile_valid = meta[2, :max_tiles]
            xs = _gather(dest_flat, n_tiles, hf.reshape(N, D), tile, max_tiles)
            tn_e = min(w_exp_gate.shape[3], 512)
            act = _grouped(_gateup_kernel, xs, (w_exp_gate, w_exp_up), i,
                           tile_expert, n_tiles, tile_valid, tile, tn_e, BF16, "moe_gate_up")
            ys = _grouped(_mm_kernel, act, (w_exp_down,), i,
                          tile_expert, n_tiles, tile_valid, tile, min(D, 1024), F32, "moe_down")
            x = _combine(dest_flat, ys, wts.reshape(B, S, LANES), x, gains, mod, l, 3, 5)
            if l + 1 < L:
                h = _norm_mod(x, gains, mod, l + 1, 0, 0, 1, BF16)
    return x
```

```python
import functools

import jax
import jax.numpy as jnp
from jax import lax
from jax.experimental import pallas as pl
from jax.experimental.pallas import tpu as pltpu

F32 = jnp.float32
BF16 = jnp.bfloat16
I32 = jnp.int32

HEAD_DIM = 128
POOL_WINDOWS = (2, 4, 8, 16)
N_MOD = 6
TOP_K = 2
RMS_EPS = 1e-6
LOG2_E = 1.4426950408889634
MOE_ROW_STEP = 128

LANES = 128
SUBLANES = 8
N_DMA_QUEUES = 2
V7X_VMEM_BYTES = 64 * 2**20
VMEM_CAP_BYTES = V7X_VMEM_BYTES - 6 * 2**20


def _params(semantics, vmem_bytes):
    limit = int(min(max(vmem_bytes * 5 // 4, 16 * 2**20), VMEM_CAP_BYTES))
    return pltpu.CompilerParams(dimension_semantics=semantics, vmem_limit_bytes=limit)


def _rms(x, gain):
    inv = lax.rsqrt(jnp.mean(x * x, axis=-1, keepdims=True) + RMS_EPS)
    return x * inv * gain


def _mod_kernel(c_ref, w_ref, b_ref, o_ref):
    c = c_ref[...]
    c_act = (c * jax.nn.sigmoid(c)).astype(BF16)
    o_ref[...] = jnp.dot(c_act, w_ref[...].astype(BF16),
                         preferred_element_type=F32) + b_ref[...]


def _mod(c, w_mod, b_mod):
    B, D = c.shape
    L, _, NM = w_mod.shape
    rows = -(-B // SUBLANES) * SUBLANES
    c_pad = jnp.pad(c, ((0, rows - B), (0, 0)))
    tn = 512
    out = pl.pallas_call(
        _mod_kernel,
        grid=(L, NM // tn),
        in_specs=[
            pl.BlockSpec((rows, D), lambda l, j: (0, 0)),
            pl.BlockSpec((None, D, tn), lambda l, j: (l, 0, j)),
            pl.BlockSpec((None, 1, tn), lambda l, j: (l, 0, j)),
        ],
        out_specs=pl.BlockSpec((None, rows, tn), lambda l, j: (l, 0, j)),
        out_shape=jax.ShapeDtypeStruct((L, rows, NM), F32),
        compiler_params=_params(("arbitrary", "arbitrary"), 3 * D * tn * 4),
        name="adaln_mod",
    )(c_pad, w_mod, b_mod.reshape(L, 1, NM))
    return out[:, :B].reshape(L, B, N_MOD, 1, D)


def _mod_spec(l, idx, D):
    return pl.BlockSpec((None, None, None, 1, D), lambda b, i: (l, b, idx, 0, 0))


def _gain_spec(l, idx, D):
    return pl.BlockSpec((None, None, 1, D), lambda b, i: (l, idx, 0, 0))


def _norm_mod_kernel(x_ref, g_ref, sc_ref, sh_ref, o_ref):
    h = _rms(x_ref[...], g_ref[...])
    o_ref[...] = (h * (1.0 + sc_ref[...]) + sh_ref[...]).astype(o_ref.dtype)


def _norm_mod(x, gains, mod, l, gain_idx, shift_idx, scale_idx, out_dtype):
    B, S, D = x.shape
    ts = min(S, 256)
    row = pl.BlockSpec((None, ts, D), lambda b, i: (b, i, 0))
    return pl.pallas_call(
        _norm_mod_kernel,
        grid=(B, S // ts),
        in_specs=[row, _gain_spec(l, gain_idx, D), _mod_spec(l, scale_idx, D),
                  _mod_spec(l, shift_idx, D)],
        out_specs=row,
        out_shape=jax.ShapeDtypeStruct((B, S, D), out_dtype),
        compiler_params=_params(("arbitrary", "arbitrary"), 4 * ts * D * 4),
        name="pre_norm",
    )(x, gains, mod, mod)


def _split_bf16(v):
    hi = v.astype(BF16)
    lo = (v - hi.astype(F32)).astype(BF16)
    return hi, lo


def _resid_norm_kernel(*refs, emit_h, emit_logits):
    x_ref, y_ref, ga_ref, gate_ref = refs[:4]
    pos = 4
    if emit_h:
        gb_ref, sc_ref, sh_ref = refs[pos:pos + 3]
        pos += 3
    if emit_logits:
        wr_ref = refs[pos]
        pos += 1
    xo_ref = refs[pos]
    pos += 1
    x_new = x_ref[...] + gate_ref[...] * _rms(y_ref[...], ga_ref[...])
    xo_ref[...] = x_new
    if emit_h:
        h_ref = refs[pos]
        pos += 1
        h = _rms(x_new, gb_ref[...]) * (1.0 + sc_ref[...]) + sh_ref[...]
        h_ref[...] = h.astype(h_ref.dtype)
        if emit_logits:
            lg_ref = refs[pos]
            h_hi, h_lo = _split_bf16(h)
            w_hi, w_lo = _split_bf16(wr_ref[...])
            lg_ref[...] = (jnp.dot(h_hi, w_hi, preferred_element_type=F32)
                           + jnp.dot(h_lo, w_hi, preferred_element_type=F32)
                           + jnp.dot(h_hi, w_lo, preferred_element_type=F32))


def _resid_norm(x, y, gains, mod, la, ga_idx, gate_idx, nxt=None, h_dtype=BF16,
                w_router=None):
    B, S, D = x.shape
    ts = min(S, 256)
    row = pl.BlockSpec((None, ts, D), lambda b, i: (b, i, 0))
    in_specs = [row, row, _gain_spec(la, ga_idx, D), _mod_spec(la, gate_idx, D)]
    args = [x, y, gains, mod]
    out_specs = [row]
    out_shape = [jax.ShapeDtypeStruct((B, S, D), F32)]
    emit_h = nxt is not None
    emit_logits = w_router is not None
    if emit_h:
        lb, gb_idx, shift_idx, scale_idx = nxt
        in_specs += [_gain_spec(lb, gb_idx, D), _mod_spec(lb, scale_idx, D),
                     _mod_spec(lb, shift_idx, D)]
        args += [gains, mod, mod]
        out_specs.append(row)
        out_shape.append(jax.ShapeDtypeStruct((B, S, D), h_dtype))
    if emit_logits:
        in_specs.append(pl.BlockSpec((D, LANES), lambda b, i: (0, 0)))
        args.append(w_router)
        out_specs.append(pl.BlockSpec((None, ts, LANES), lambda b, i: (b, i, 0)))
        out_shape.append(jax.ShapeDtypeStruct((B, S, LANES), F32))
    outs = pl.pallas_call(
        functools.partial(_resid_norm_kernel, emit_h=emit_h, emit_logits=emit_logits),
        grid=(B, S // ts),
        in_specs=in_specs,
        out_specs=out_specs,
        out_shape=out_shape,
        compiler_params=_params(("arbitrary", "arbitrary"), 10 * ts * D * 4),
        name="resid_norm",
    )(*args)
    return outs


def _mm_kernel(a_ref, w_ref, o_ref):
    o_ref[...] = jnp.dot(a_ref[...], w_ref[...].astype(BF16),
                         preferred_element_type=F32).astype(o_ref.dtype)


def _nt_dot(a, wt):
    return lax.dot_general(a, wt, (((1,), (1,)), ((), ())), preferred_element_type=F32)


def _mm_nt_kernel(a_ref, wt_ref, o_ref):
    o_ref[...] = _nt_dot(a_ref[...], wt_ref[...].astype(BF16)).astype(o_ref.dtype)


def _qkv_kernel(a_ref, wt_ref, cs_ref, o_ref):
    acc = _nt_dot(a_ref[...], wt_ref[...].astype(BF16)) * cs_ref[...]
    for hh in range(o_ref.shape[0]):
        o_ref[hh] = acc[:, hh * HEAD_DIM:(hh + 1) * HEAD_DIM].astype(o_ref.dtype)


def _mm_two_kernel(a1_ref, a2_ref, w_ref, o_ref):
    k1 = a1_ref.shape[1]
    w = w_ref[...].astype(BF16)
    o_ref[...] = (jnp.dot(a1_ref[...], w[:k1], preferred_element_type=F32)
                  + jnp.dot(a2_ref[...], w[k1:], preferred_element_type=F32)).astype(o_ref.dtype)


def _gateup_kernel(a_ref, wg_ref, wu_ref, o_ref):
    a = a_ref[...]
    g = jnp.dot(a, wg_ref[...].astype(BF16), preferred_element_type=F32)
    u = jnp.dot(a, wu_ref[...].astype(BF16), preferred_element_type=F32)
    o_ref[...] = (g * jax.nn.sigmoid(g) * u).astype(o_ref.dtype)


def _w_spec(w, tn, lead):
    K = w.shape[-2]
    nlead = len(lead)
    return pl.BlockSpec((None,) * nlead + (K, tn), lambda i, j: tuple(lead) + (0, j))


def _a_spec(tm, K):
    return pl.BlockSpec((tm, K), lambda i, j: (i, 0), pipeline_mode=pl.Buffered(1))


def _dense_vmem(tm, K, tn, n_w, w_bytes, out_bytes, a_buffers):
    a = a_buffers * tm * K * 2
    w = n_w * (2 * K * tn * w_bytes + K * tn * 2)
    o = 2 * tm * tn * out_bytes + n_w * tm * tn * 4
    return a + w + o


def _qkv_proj(a, wt, l, n_cols, tm, tn, colscale):
    M, K = a.shape
    hpb = tn // HEAD_DIM
    return pl.pallas_call(
        _qkv_kernel,
        grid=(M // tm, n_cols // tn),
        in_specs=[_a_spec(tm, K),
                  pl.BlockSpec((None, tn, K), lambda i, j: (l, j, 0)),
                  pl.BlockSpec((1, tn), lambda i, j: (0, j))],
        out_specs=pl.BlockSpec((hpb, tm, HEAD_DIM), lambda i, j: (j, i, 0)),
        out_shape=jax.ShapeDtypeStruct((n_cols // HEAD_DIM, M, HEAD_DIM), BF16),
        compiler_params=_params(("arbitrary", "arbitrary"),
                                _dense_vmem(tm, K, tn, 1, wt.dtype.itemsize, 2, 1)),
        name="in_proj_qkv",
    )(a, wt, colscale)


def _matmul_nt(a, wt, tm, tn, out_dtype, name):
    M, K = a.shape
    n_cols = wt.shape[0]
    return pl.pallas_call(
        _mm_nt_kernel,
        grid=(M // tm, n_cols // tn),
        in_specs=[pl.BlockSpec((tm, K), lambda i, j: (i, 0)),
                  pl.BlockSpec((tn, K), lambda i, j: (j, 0))],
        out_specs=pl.BlockSpec((tm, tn), lambda i, j: (i, j)),
        out_shape=jax.ShapeDtypeStruct((M, n_cols), out_dtype),
        compiler_params=_params(
            ("arbitrary", "arbitrary"),
            _dense_vmem(tm, K, tn, 1, wt.dtype.itemsize, jnp.dtype(out_dtype).itemsize, 2)),
        name=name,
    )(a, wt)


def _matmul_two(a1, a2, w, lead, tm, tn, out_dtype, name):
    M, K1 = a1.shape
    K2 = a2.shape[1]
    n_cols = w.shape[-1]
    return pl.pallas_call(
        _mm_two_kernel,
        grid=(M // tm, n_cols // tn),
        in_specs=[_a_spec(tm, K1), _a_spec(tm, K2), _w_spec(w, tn, lead)],
        out_specs=pl.BlockSpec((tm, tn), lambda i, j: (i, j)),
        out_shape=jax.ShapeDtypeStruct((M, n_cols), out_dtype),
        compiler_params=_params(
            ("arbitrary", "arbitrary"),
            _dense_vmem(tm, K1 + K2, tn, 1, w.dtype.itemsize, jnp.dtype(out_dtype).itemsize, 1)),
        name=name,
    )(a1, a2, w)


def _gateup(a, wg, wu, lead, tm, tn, name="ffn_gate_up"):
    M, K = a.shape
    n_cols = wg.shape[-1]
    return pl.pallas_call(
        _gateup_kernel,
        grid=(M // tm, n_cols // tn),
        in_specs=[_a_spec(tm, K), _w_spec(wg, tn, lead), _w_spec(wu, tn, lead)],
        out_specs=pl.BlockSpec((tm, tn), lambda i, j: (i, j)),
        out_shape=jax.ShapeDtypeStruct((M, n_cols), BF16),
        compiler_params=_params(("arbitrary", "arbitrary"),
                                _dense_vmem(tm, K, tn, 2, wg.dtype.itemsize, 2, 1)),
        name=name,
    )(a, wg, wu)


def _down(a, w, lead, tm, tn, name="ffn_down"):
    M, K = a.shape
    n_cols = w.shape[-1]
    return pl.pallas_call(
        _mm_kernel,
        grid=(M // tm, n_cols // tn),
        in_specs=[_a_spec(tm, K), _w_spec(w, tn, lead)],
        out_specs=pl.BlockSpec((tm, tn), lambda i, j: (i, j)),
        out_shape=jax.ShapeDtypeStruct((M, n_cols), F32),
        compiler_params=_params(("arbitrary", "arbitrary"),
                                _dense_vmem(tm, K, tn, 1, w.dtype.itemsize, 4, 1)),
        name=name,
    )(a, w)


def _fox_prep_kernel(f_ref, b_ref, fsh_ref, fhs_ref):
    z = f_ref[...] + b_ref[...]
    acc = jnp.minimum(z, 0.0) - jnp.log1p(jnp.exp(-jnp.abs(z)))
    S = acc.shape[0]
    row = lax.broadcasted_iota(I32, acc.shape, 0)
    d = 1
    while d < S:
        acc = acc + jnp.where(row >= d, pltpu.roll(acc, d, 0), 0.0)
        d *= 2
    acc = acc * LOG2_E
    fsh_ref[...] = acc
    fhs_ref[...] = acc.T


def _fox_prep(fu3, b_pad, col_block):
    B, S, _ = fu3.shape
    return pl.pallas_call(
        _fox_prep_kernel,
        grid=(B,),
        in_specs=[pl.BlockSpec((None, S, LANES), lambda b: (b, 0, col_block)),
                  pl.BlockSpec((1, LANES), lambda b: (0, 0))],
        out_specs=[pl.BlockSpec((None, S, LANES), lambda b: (b, 0, 0)),
                   pl.BlockSpec((None, LANES, S), lambda b: (b, 0, 0))],
        out_shape=[jax.ShapeDtypeStruct((B, S, LANES), F32),
                   jax.ShapeDtypeStruct((B, LANES, S), F32)],
        compiler_params=_params(("arbitrary",), 12 * S * LANES * 4),
        name="fox_prep",
    )(fu3, b_pad)


def _attn_kernel(q_ref, k_ref, v_ref, fq_ref, fk_ref, o_ref, fq_s, *, blk):
    heads = q_ref.shape[0]
    for hh in range(heads):
        _attn_head(pl.program_id(1) * heads + hh, q_ref.at[hh], k_ref.at[hh], v_ref.at[hh],
                   fq_ref, fk_ref.at[hh], o_ref, hh * HEAD_DIM, fq_s.at[hh], blk)


def _attn_head(h, q_ref, k_ref, v_ref, fq_ref, fk_ref, o_ref, o_col, fq_s, blk):
    S = q_ref.shape[0]
    n_blk = S // blk
    lane = lax.broadcasted_iota(I32, fq_ref.shape, 1)
    fq_col = jnp.sum(jnp.where(lane == h, fq_ref[...], 0.0), axis=-1, keepdims=True)
    fq_s[...] = jnp.broadcast_to(fq_col, fq_s.shape)

    def logits(row_lo, n_rows, key_lo, n_keys):
        q = q_ref[row_lo:row_lo + n_rows, :]
        fq = fq_s[row_lo:row_lo + n_rows, :]
        s = lax.dot_general(q, k_ref[key_lo:key_lo + n_keys, :], (((1,), (1,)), ((), ())),
                            preferred_element_type=F32)
        chunks = []
        for c in range(n_keys // LANES):
            lo = key_lo + c * LANES
            sc = s[:, c * LANES:(c + 1) * LANES] + (fq - fk_ref[:, lo:lo + LANES])
            if lo + LANES - 1 > row_lo:
                r = lax.broadcasted_iota(I32, (n_rows, LANES), 0) + row_lo
                k = lax.broadcasted_iota(I32, (n_rows, LANES), 1) + lo
                sc = jnp.where(k <= r, sc, -jnp.inf)
            chunks.append(sc)
        return chunks

    def online_update(state, chunks, v):
        m_row = jnp.max(functools.reduce(jnp.maximum, chunks), axis=-1, keepdims=True)
        m_new = m_row if state is None else jnp.maximum(state[0], m_row)
        m_new = jnp.broadcast_to(m_new, chunks[0].shape)
        ps = [jnp.exp2(sc - m_new) for sc in chunks]
        p_sum = functools.reduce(lambda a, b: a + b, ps)
        pv = jnp.dot(jnp.concatenate(ps, axis=1).astype(BF16), v, preferred_element_type=F32)
        if state is None:
            return m_new, p_sum, pv
        alpha = jnp.exp2(state[0] - m_new)
        return m_new, alpha * state[1] + p_sum, alpha * state[2] + pv

    for i in range(n_blk):
        r0 = i * blk
        state = None
        for j in range(i):
            k0 = j * blk
            state = online_update(state, logits(r0, blk, k0, blk), v_ref[k0:k0 + blk, :])
        _, l_b, acc = online_update(state, logits(r0, blk, r0, blk), v_ref[r0:r0 + blk, :])
        l_row = jnp.sum(l_b, axis=-1, keepdims=True)
        o_ref[r0:r0 + blk, o_col:o_col + HEAD_DIM] = (acc / l_row).astype(o_ref.dtype)


def _attention(qkv_h, f_sh, f_hs4, n_heads):
    _, B, S, _ = qkv_h.shape
    H = n_heads
    blk = min(S, 512)
    hps = 2 if H % 2 == 0 else 1
    head = lambda off: pl.BlockSpec((hps, None, S, HEAD_DIM),
                                    lambda b, g: (off // hps + g, b, 0, 0))
    return pl.pallas_call(
        functools.partial(_attn_kernel, blk=blk),
        grid=(B, H // hps),
        in_specs=[head(0), head(H), head(2 * H),
                  pl.BlockSpec((None, S, LANES), lambda b, g: (b, 0, 0)),
                  pl.BlockSpec((None, hps, 1, S), lambda b, g: (b, g, 0, 0))],
        out_specs=pl.BlockSpec((None, S, hps * HEAD_DIM), lambda b, g: (b, 0, g)),
        out_shape=jax.ShapeDtypeStruct((B, S, H * HEAD_DIM), BF16),
        scratch_shapes=[pltpu.VMEM((hps, S, LANES), F32)],
        compiler_params=_params(
            ("arbitrary", "arbitrary"),
            hps * (8 * S * HEAD_DIM * 2 + 3 * S * LANES * 4 + 8 * blk * blk * 4)),
        name="fox_attention",
    )(qkv_h, qkv_h, qkv_h, f_sh, f_hs4)


def _pool_kernel(u_ref, w_ref, sc_ref, o_ref):
    g = pl.program_id(1)
    u = u_ref[...]
    row = lax.broadcasted_iota(I32, u.shape, 0)

    def shifted(v, d):
        return jnp.where(row >= d, pltpu.roll(v, d, 0), 0.0)

    s2 = u + shifted(u, 1)
    s4 = s2 + shifted(s2, 2)
    s8 = s4 + shifted(s4, 4)
    s16 = s8 + shifted(s8, 8)
    win = jnp.where(g == 0, s2, jnp.where(g == 1, s4, jnp.where(g == 2, s8, s16)))
    width = lax.shift_left(jnp.int32(POOL_WINDOWS[0]), g)
    count = jnp.minimum(row + 1, width).astype(F32)
    pooled = win / count - u
    y = jnp.dot(pooled.astype(BF16), w_ref[...].astype(BF16), preferred_element_type=F32)
    o_ref[...] = (y * sc_ref[...]).astype(o_ref.dtype)


def _pool(fu3, w_pool, pool_scale3, l):
    B, S, _ = fu3.shape
    G, C = w_pool.shape[1], w_pool.shape[2]
    assert G == len(POOL_WINDOWS) and all(
        w == POOL_WINDOWS[0] << k for k, w in enumerate(POOL_WINDOWS))
    return pl.pallas_call(
        _pool_kernel,
        grid=(B, G),
        in_specs=[pl.BlockSpec((None, S, C), lambda b, g: (b, 0, g)),
                  pl.BlockSpec((None, None, C, C), lambda b, g: (l, g, 0, 0)),
                  pl.BlockSpec((None, 1, C), lambda b, g: (l, 0, g))],
        out_specs=pl.BlockSpec((None, S, C), lambda b, g: (b, 0, g)),
        out_shape=jax.ShapeDtypeStruct((B, S, G * C), BF16),
        compiler_params=_params(("arbitrary", "arbitrary"), 14 * S * C * 4),
        name="pool_mixer",
    )(fu3, w_pool, pool_scale3)


def _route_kernel(lg_ref, dest_ref, wts_ref, meta_ref, m_all, m_top, tri, *,
                  n_experts, tile, chunk):
    N = lg_ref.shape[0]
    n_chunks = N // chunk
    lane = lax.broadcasted_iota(I32, (chunk, LANES), 1).astype(F32)
    lane1 = lax.broadcasted_iota(I32, (1, LANES), 1).astype(F32)
    r = lax.broadcasted_iota(I32, (chunk, chunk), 0)
    c = lax.broadcasted_iota(I32, (chunk, chunk), 1)
    tri[...] = jnp.where(c < r, 1.0, 0.0).astype(BF16)

    def select(ci, counts):
        rows = pl.ds(pl.multiple_of(ci * chunk, chunk), chunk)
        lg = jnp.where(lane < n_experts, lg_ref[rows, :], -jnp.inf)
        v1 = jnp.max(lg, axis=-1, keepdims=True)
        i1 = jnp.min(jnp.where(lg == v1, lane, float(LANES)), axis=-1, keepdims=True)
        first = lane == i1
        lg2 = jnp.where(first, -jnp.inf, lg)
        v2 = jnp.max(lg2, axis=-1, keepdims=True)
        i2 = jnp.min(jnp.where(lg2 == v2, lane, float(LANES)), axis=-1, keepdims=True)
        second = lane == i2
        e2 = jnp.exp(v2 - v1)
        w1 = 1.0 / (1.0 + e2)
        w2 = e2 / (1.0 + e2)
        wts_ref[rows, :] = jnp.where(lane == 0.0, w1, jnp.where(lane == 1.0, w2, 0.0))
        both = jnp.where(first | second, 1.0, 0.0)
        m_all[rows, :] = both.astype(BF16)
        m_top[rows, :] = jnp.where(first, 1.0, 0.0).astype(BF16)
        return counts + jnp.sum(both, axis=0, keepdims=True)

    counts = lax.fori_loop(0, n_chunks, select, jnp.zeros((1, LANES), F32))
    padded = jnp.ceil(counts / tile) * tile
    start = jnp.zeros((1, LANES), F32)
    tile_row = lane1 * tile
    tile_expert = jnp.zeros((1, LANES), F32)
    last_used = jnp.zeros((1, 1), F32)
    running = jnp.zeros((1, 1), F32)
    tile_valid = jnp.zeros((1, LANES), F32)
    for e in range(n_experts):
        rows_e = jnp.sum(jnp.where(lane1 == e, padded, 0.0), axis=-1, keepdims=True)
        count_e = jnp.sum(jnp.where(lane1 == e, counts, 0.0), axis=-1, keepdims=True)
        in_group = (tile_row >= running) & (tile_row < running + rows_e)
        tile_valid = jnp.where(in_group, jnp.minimum(running + count_e - tile_row, float(tile)),
                               tile_valid)
        start = start + jnp.where(lane1 > e, rows_e, 0.0)
        running = running + rows_e
        tile_expert = tile_expert + jnp.where(tile_row >= running, 1.0, 0.0)
        last_used = jnp.where(rows_e > 0.0, float(e), last_used)
    tile_expert = jnp.minimum(tile_expert, last_used)
    n_tiles = running / tile
    sub = lax.broadcasted_iota(I32, (SUBLANES, LANES), 0)
    meta_ref[...] = jnp.where(sub == 0, tile_expert,
                              jnp.where(sub == 1, n_tiles,
                                        jnp.where(sub == 2, tile_valid, 0.0))).astype(I32)

    def place(ci, carry):
        rows = pl.ds(pl.multiple_of(ci * chunk, chunk), chunk)
        both = m_all[rows, :]
        top = m_top[rows, :].astype(F32)
        rank = jnp.dot(tri[...], both, preferred_element_type=F32)
        pos = start + carry + rank
        both_f = both.astype(F32)
        d1 = jnp.sum(top * pos, axis=-1, keepdims=True)
        d2 = jnp.sum((both_f - top) * pos, axis=-1, keepdims=True)
        dest_ref[rows, :] = jnp.where(lane == 0.0, d1, jnp.where(lane == 1.0, d2, 0.0)).astype(I32)
        return carry + jnp.sum(both_f, axis=0, keepdims=True)

    lax.fori_loop(0, n_chunks, place, jnp.zeros((1, LANES), F32))


def _route(logits, n_experts, tile):
    N = logits.shape[0]
    chunk = min(N, 512)
    return pl.pallas_call(
        functools.partial(_route_kernel, n_experts=n_experts, tile=tile, chunk=chunk),
        out_shape=[jax.ShapeDtypeStruct((N, LANES), I32),
                   jax.ShapeDtypeStruct((N, LANES), F32),
                   jax.ShapeDtypeStruct((SUBLANES, LANES), I32)],
        scratch_shapes=[pltpu.VMEM((N, LANES), BF16), pltpu.VMEM((N, LANES), BF16),
                        pltpu.VMEM((chunk, chunk), BF16)],
        compiler_params=pltpu.CompilerParams(
            vmem_limit_bytes=int(min(max(N * LANES * 4 * 8, 16 * 2**20), VMEM_CAP_BYTES))),
        name="moe_route",
    )(logits)


def _gather_kernel(dest_ref, nt_ref, h_hbm, o_ref, src, stage, sems, *, tile, n_rows, n_assign):
    t = pl.program_id(0)
    n_used = nt_ref[0]
    slot = t % 2

    def issue_tile(tt, to_slot):
        base = tt * tile

        def issue(pair, carry):
            for queue in range(N_DMA_QUEUES):
                rr = pair * N_DMA_QUEUES + queue
                tok = src[base + rr]
                pltpu.make_async_copy(h_hbm.at[pl.ds(tok, 1), :],
                                      stage.at[to_slot, pl.ds(rr, 1), :],
                                      sems.at[to_slot]).start(priority=queue)
            return carry
        lax.fori_loop(0, tile // N_DMA_QUEUES, issue, 0, unroll=4)

    @pl.when(t == 0)
    def _invert_dest():
        def clear(rr, carry):
            src[rr] = 0
            return carry
        lax.fori_loop(0, n_rows, clear, 0, unroll=8)

        def scatter(n, carry):
            for k in range(TOP_K):
                src[dest_ref[n * TOP_K + k]] = n
            return carry
        lax.fori_loop(0, n_assign // TOP_K, scatter, 0, unroll=4)
        issue_tile(0, 0)

    @pl.when(t + 1 < n_used)
    def _prefetch_next():
        issue_tile(t + 1, 1 - slot)

    @pl.when(t < n_used)
    def _gather():
        pltpu.make_async_copy(h_hbm.at[pl.ds(0, tile), :], stage.at[slot], sems.at[slot]).wait()
        o_ref[...] = stage[slot].astype(o_ref.dtype)

    @pl.when(t >= n_used)
    def _unused_tile():
        o_ref[...] = jnp.zeros(o_ref.shape, o_ref.dtype)


def _gather(dest_flat, n_tiles, h2, tile, max_tiles):
    N, D = h2.shape
    n_rows = max_tiles * tile
    grid_spec = pltpu.PrefetchScalarGridSpec(
        num_scalar_prefetch=2,
        grid=(max_tiles,),
        in_specs=[pl.BlockSpec(memory_space=pl.ANY)],
        out_specs=pl.BlockSpec((tile, D), lambda t, dest, nt: (t, 0)),
        scratch_shapes=[pltpu.SMEM((n_rows,), I32), pltpu.VMEM((2, tile, D), F32),
                        pltpu.SemaphoreType.DMA((2,))],
    )
    return pl.pallas_call(
        functools.partial(_gather_kernel, tile=tile, n_rows=n_rows, n_assign=dest_flat.shape[0]),
        grid_spec=grid_spec,
        out_shape=jax.ShapeDtypeStruct((n_rows, D), BF16),
        compiler_params=_params(("arbitrary",), 4 * tile * D * 4),
        name="moe_gather",
    )(dest_flat, n_tiles, h2)


def _grouped_kernel(te_ref, nt_ref, nv_ref, a_ref, *rest, n_w, layer, tn, max_tiles, compute):
    w_hbm = rest[:n_w]
    o_ref = rest[n_w]
    w_buf = rest[n_w + 1:2 * n_w + 1]
    sems, state = rest[2 * n_w + 1:]
    j = pl.program_id(0)
    t = pl.program_id(1)
    n_used = nt_ref[0]

    def fetch(e, jj, slot):
        return [pltpu.make_async_copy(w_hbm[wi].at[layer, e, :, pl.ds(jj * tn, tn)],
                                      w_buf[wi].at[slot], sems.at[wi, slot])
                for wi in range(n_w)]

    @pl.when((j == 0) & (t == 0))
    def _first_fetch():
        state[0] = 0
        for cp in fetch(te_ref[0], 0, 0):
            cp.start()

    e = te_ref[t]
    run_start = (t < n_used) & ((t == 0) | (e != te_ref[jnp.maximum(t - 1, 0)]))

    @pl.when(run_start)
    def _swap_weights():
        slot = state[0]
        for cp in fetch(e, j, slot):
            cp.wait()
        t_next = lax.while_loop(
            lambda tt: (tt < n_used) & (te_ref[jnp.minimum(tt, max_tiles - 1)] == e),
            lambda tt: tt + 1, t + 1)
        same_sweep = t_next < n_used

        @pl.when(same_sweep)
        def _next_expert():
            for cp in fetch(te_ref[jnp.minimum(t_next, max_tiles - 1)], j, 1 - slot):
                cp.start()

        @pl.when(jnp.logical_not(same_sweep) & (j + 1 < pl.num_programs(0)))
        def _next_sweep():
            for cp in fetch(te_ref[0], j + 1, 1 - slot):
                cp.start()

        state[1] = slot
        state[0] = 1 - slot

    @pl.when(t < n_used)
    def _compute():
        slot = state[1]
        weights = [wb.at[slot] for wb in w_buf]
        n_valid = nv_ref[t]
        tile = a_ref.shape[0]
        for rows in range(MOE_ROW_STEP, tile + 1, MOE_ROW_STEP):
            @pl.when((n_valid > rows - MOE_ROW_STEP) & (n_valid <= rows))
            def _leading_rows(rows=rows):
                compute(a_ref.at[pl.ds(0, rows)], *weights, o_ref.at[pl.ds(0, rows)])
                if rows < tile:
                    o_ref[pl.ds(rows, tile - rows), :] = jnp.zeros((tile - rows, tn), o_ref.dtype)

    @pl.when(t >= n_used)
    def _unused_tile():
        o_ref[...] = jnp.zeros(o_ref.shape, o_ref.dtype)


def _grouped(compute, a, ws, l, tile_expert, n_tiles, tile_valid, tile, tn, out_dtype, name):
    R, K = a.shape
    n_cols = ws[0].shape[-1]
    max_tiles = R // tile
    n_w = len(ws)
    assert tile % MOE_ROW_STEP == 0
    grid_spec = pltpu.PrefetchScalarGridSpec(
        num_scalar_prefetch=3,
        grid=(n_cols // tn, max_tiles),
        in_specs=[pl.BlockSpec((tile, K),
                               lambda j, t, te, nt, nv: (jnp.minimum(t, nt[0] - 1), 0))]
        + [pl.BlockSpec(memory_space=pl.ANY)] * n_w,
        out_specs=pl.BlockSpec((tile, tn), lambda j, t, te, nt, nv: (t, j)),
        scratch_shapes=[pltpu.VMEM((2, K, tn), ws[0].dtype) for _ in range(n_w)]
        + [pltpu.SemaphoreType.DMA((n_w, 2)), pltpu.SMEM((2,), I32)],
    )
    return pl.pallas_call(
        functools.partial(_grouped_kernel, n_w=n_w, layer=l, tn=tn, max_tiles=max_tiles,
                          compute=compute),
        grid_spec=grid_spec,
        out_shape=jax.ShapeDtypeStruct((R, n_cols), out_dtype),
        compiler_params=_params(
            ("arbitrary", "arbitrary"),
            _dense_vmem(tile, K, tn, n_w, ws[0].dtype.itemsize,
                        jnp.dtype(out_dtype).itemsize, 2)),
        name=name,
    )(tile_expert, n_tiles, tile_valid, a, *ws)


def _combine_kernel(dest_ref, ys_hbm, wts_ref, x_ref, g_ref, gate_ref, o_ref,
                    stage, sems, *, tile, tiles_per_seq, n_steps):
    t = pl.program_id(0) * tiles_per_seq + pl.program_id(1)
    slot = t % 2

    def issue_tile(tt, to_slot):
        n0 = tt * tile

        def issue(rr, carry):
            a = (n0 + rr) * TOP_K
            for k in range(TOP_K):
                pltpu.make_async_copy(ys_hbm.at[pl.ds(dest_ref[a + k], 1), :],
                                      stage.at[to_slot, k, pl.ds(rr, 1), :],
                                      sems.at[to_slot, k]).start(priority=k % N_DMA_QUEUES)
            return carry
        lax.fori_loop(0, tile, issue, 0, unroll=4)

    @pl.when(t == 0)
    def _first_tile():
        issue_tile(0, 0)

    @pl.when(t + 1 < n_steps)
    def _prefetch_next():
        issue_tile(t + 1, 1 - slot)

    for k in range(TOP_K):
        pltpu.make_async_copy(ys_hbm.at[pl.ds(0, tile), :], stage.at[slot, k],
                              sems.at[slot, k]).wait()
    w = wts_ref[...]
    y = w[:, 0:1] * stage[slot, 0] + w[:, 1:2] * stage[slot, 1]
    o_ref[...] = x_ref[...] + gate_ref[...] * _rms(y, g_ref[...])


def _combine(dest_flat, ys, wts3, x, gains, mod, l, g_idx, gate_idx):
    B, S, D = x.shape
    tile = min(S, 256)
    row = pl.BlockSpec((None, tile, D), lambda b, i, dest: (b, i, 0))
    grid_spec = pltpu.PrefetchScalarGridSpec(
        num_scalar_prefetch=1,
        grid=(B, S // tile),
        in_specs=[pl.BlockSpec(memory_space=pl.ANY),
                  pl.BlockSpec((None, tile, LANES), lambda b, i, dest: (b, i, 0)),
                  row,
                  pl.BlockSpec((None, None, 1, D), lambda b, i, dest: (l, g_idx, 0, 0)),
                  pl.BlockSpec((None, None, None, 1, D),
                               lambda b, i, dest: (l, b, gate_idx, 0, 0))],
        out_specs=row,
        scratch_shapes=[pltpu.VMEM((2, TOP_K, tile, D), F32),
                        pltpu.SemaphoreType.DMA((2, TOP_K))],
    )
    return pl.pallas_call(
        functools.partial(_combine_kernel, tile=tile, tiles_per_seq=S // tile,
                          n_steps=B * (S // tile)),
        grid_spec=grid_spec,
        out_shape=jax.ShapeDtypeStruct((B, S, D), F32),
        compiler_params=_params(("arbitrary", "arbitrary"), 12 * tile * D * 4),
        name="moe_combine",
    )(dest_flat, ys, wts3, x, gains, mod)


def kernel(x, c, norm_gains, w_mod, b_mod, w_in, b_forget, w_pool, pool_scale, w_out,
           w_ffn_gate, w_ffn_up, w_ffn_down, w_router, w_exp_gate, w_exp_up, w_exp_down):
    B, S, D = x.shape
    L = w_mod.shape[0]
    N = B * S
    H = b_forget.shape[1]
    A = H * HEAD_DIM
    P = pool_scale.shape[1]
    assert A + P == D and w_in.shape[2] == 3 * A + H + P and H <= LANES
    assert P % LANES == 0 and A % LANES == 0
    E = w_router.shape[2]
    tm = min(N, 1024)
    tm_wide = min(N, 2048)
    tn_fu = 2 * LANES
    fu_cols = -(-(P + LANES) // tn_fu) * tn_fu

    mod = _mod(c, w_mod, b_mod)
    gains = norm_gains.reshape(L, 4, 1, D)
    h = _norm_mod(x, gains, mod, 0, 0, 0, 1, BF16)

    qk_scale = jnp.concatenate([jnp.full((1, A), HEAD_DIM ** -0.5 * LOG2_E, F32),
                                jnp.ones((1, 2 * A), F32)], axis=1)
    pool_scale3 = pool_scale.reshape(L, 1, P)
    w_in_t = jnp.swapaxes(w_in, 1, 2)

    for l in range(L):
        h2 = h.reshape(N, D)
        qkv_h = _qkv_proj(h2, w_in_t, l, 3 * A, tm_wide, min(3 * A, 512), qk_scale)
        w_fu_t = jnp.concatenate([w_in_t[l, 3 * A + H:], w_in_t[l, 3 * A:3 * A + H],
                                  jnp.zeros((fu_cols - P - H, D), F32)], axis=0)
        fu = _matmul_nt(h2, w_fu_t, tm, tn_fu, F32, "in_proj_gate_pool")
        fu3 = fu.reshape(B, S, fu_cols)
        b_pad = jnp.pad(b_forget[l], (0, LANES - H)).reshape(1, LANES)
        f_sh, f_hs = _fox_prep(fu3, b_pad, P // LANES)
        attn = _attention(qkv_h.reshape(3 * H, B, S, HEAD_DIM), f_sh,
                          f_hs.reshape(B, LANES, 1, S), H)
        pooled = _pool(fu3, w_pool, pool_scale3, l)
        mixed = _matmul_two(attn.reshape(N, A), pooled.reshape(N, P), w_out, (l,), tm_wide,
                            min(D, 512), F32, "out_proj")

        i = l // 2
        if l % 2 == 0:
            x, h = _resid_norm(x, mixed.reshape(B, S, D), gains, mod, l, 1, 2,
                               nxt=(l, 2, 3, 4))
            FF = w_ffn_gate.shape[2]
            tn_ff = 256 if FF % 256 == 0 else LANES
            act = _gateup(h.reshape(N, D), w_ffn_gate, w_ffn_up, (i,), tm_wide, tn_ff)
            y = _down(act, w_ffn_down, (i,), tm, min(D, 256))
            y3 = y.reshape(B, S, D)
            if l + 1 < L:
                x, h = _resid_norm(x, y3, gains, mod, l, 3, 5, nxt=(l + 1, 0, 0, 1))
            else:
                (x,) = _resid_norm(x, y3, gains, mod, l, 3, 5)
        else:
            wr_pad = jnp.pad(w_router[i], ((0, 0), (0, LANES - E)))
            x, hf, logits = _resid_norm(x, mixed.reshape(B, S, D), gains, mod, l, 1, 2,
                                        nxt=(l, 2, 3, 4), h_dtype=F32, w_router=wr_pad)
            tile = min(N, 512)
            max_tiles = (N * TOP_K) // tile + E
            assert max_tiles <= LANES
            dest, wts, meta = _route(logits.reshape(N, LANES), E, tile)
            dest_flat = dest[:, :TOP_K].reshape(N * TOP_K)
            tile_expert = meta[0, :max_tiles]
            n_tiles = meta[1, :1]
            tile_valid = meta[2, :max_tiles]
            xs = _gather(dest_flat, n_tiles, hf.reshape(N, D), tile, max_tiles)
            tn_e = min(w_exp_gate.shape[3], 512)
            act = _grouped(_gateup_kernel, xs, (w_exp_gate, w_exp_up), i,
                           tile_expert, n_tiles, tile_valid, tile, tn_e, BF16, "moe_gate_up")
            ys = _grouped(_mm_kernel, act, (w_exp_down,), i,
                          tile_expert, n_tiles, tile_valid, tile, min(D, 1024), F32, "moe_down")
            x = _combine(dest_flat, ys, wts.reshape(B, S, LANES), x, gains, mod, l, 3, 5)
            if l + 1 < L:
                h = _norm_mod(x, gains, mod, l + 1, 0, 0, 1, BF16)
    return x
```

```python
import functools

import jax
import jax.numpy as jnp
from jax import lax
from jax.experimental import pallas as pl
from jax.experimental.pallas import tpu as pltpu

F32 = jnp.float32
BF16 = jnp.bfloat16
I32 = jnp.int32

HEAD_DIM = 128
POOL_WINDOWS = (2, 4, 8, 16)
N_MOD = 6
TOP_K = 2
RMS_EPS = 1e-6
LOG2_E = 1.4426950408889634
MOE_ROW_STEP = 128

LANES = 128
SUBLANES = 8
V7X_VMEM_BYTES = 64 * 2**20
VMEM_CAP_BYTES = V7X_VMEM_BYTES - 6 * 2**20


def _params(semantics, vmem_bytes):
    limit = int(min(max(vmem_bytes * 5 // 4, 16 * 2**20), VMEM_CAP_BYTES))
    return pltpu.CompilerParams(dimension_semantics=semantics, vmem_limit_bytes=limit)


def _rms(x, gain):
    inv = lax.rsqrt(jnp.mean(x * x, axis=-1, keepdims=True) + RMS_EPS)
    return x * inv * gain


def _mod_kernel(c_ref, w_ref, b_ref, o_ref):
    c = c_ref[...]
    c_act = (c * jax.nn.sigmoid(c)).astype(BF16)
    o_ref[...] = jnp.dot(c_act, w_ref[...].astype(BF16),
                         preferred_element_type=F32) + b_ref[...]


def _mod(c, w_mod, b_mod):
    B, D = c.shape
    L, _, NM = w_mod.shape
    rows = -(-B // SUBLANES) * SUBLANES
    c_pad = jnp.pad(c, ((0, rows - B), (0, 0)))
    tn = 512
    out = pl.pallas_call(
        _mod_kernel,
        grid=(L, NM // tn),
        in_specs=[
            pl.BlockSpec((rows, D), lambda l, j: (0, 0)),
            pl.BlockSpec((None, D, tn), lambda l, j: (l, 0, j)),
            pl.BlockSpec((None, 1, tn), lambda l, j: (l, 0, j)),
        ],
        out_specs=pl.BlockSpec((None, rows, tn), lambda l, j: (l, 0, j)),
        out_shape=jax.ShapeDtypeStruct((L, rows, NM), F32),
        compiler_params=_params(("arbitrary", "arbitrary"), 3 * D * tn * 4),
        name="adaln_mod",
    )(c_pad, w_mod, b_mod.reshape(L, 1, NM))
    return out[:, :B].reshape(L, B, N_MOD, 1, D)


def _mod_spec(l, idx, D):
    return pl.BlockSpec((None, None, None, 1, D), lambda b, i: (l, b, idx, 0, 0))


def _gain_spec(l, idx, D):
    return pl.BlockSpec((None, None, 1, D), lambda b, i: (l, idx, 0, 0))


def _norm_mod_kernel(x_ref, g_ref, sc_ref, sh_ref, o_ref):
    h = _rms(x_ref[...], g_ref[...])
    o_ref[...] = (h * (1.0 + sc_ref[...]) + sh_ref[...]).astype(o_ref.dtype)


def _norm_mod(x, gains, mod, l, gain_idx, shift_idx, scale_idx, out_dtype):
    B, S, D = x.shape
    ts = min(S, 256)
    row = pl.BlockSpec((None, ts, D), lambda b, i: (b, i, 0))
    return pl.pallas_call(
        _norm_mod_kernel,
        grid=(B, S // ts),
        in_specs=[row, _gain_spec(l, gain_idx, D), _mod_spec(l, scale_idx, D),
                  _mod_spec(l, shift_idx, D)],
        out_specs=row,
        out_shape=jax.ShapeDtypeStruct((B, S, D), out_dtype),
        compiler_params=_params(("arbitrary", "arbitrary"), 4 * ts * D * 4),
        name="pre_norm",
    )(x, gains, mod, mod)


def _split_bf16(v):
    hi = v.astype(BF16)
    lo = (v - hi.astype(F32)).astype(BF16)
    return hi, lo


def _resid_norm_kernel(*refs, emit_h, emit_logits):
    x_ref, y_ref, ga_ref, gate_ref = refs[:4]
    pos = 4
    if emit_h:
        gb_ref, sc_ref, sh_ref = refs[pos:pos + 3]
        pos += 3
    if emit_logits:
        wr_ref = refs[pos]
        pos += 1
    xo_ref = refs[pos]
    pos += 1
    x_new = x_ref[...] + gate_ref[...] * _rms(y_ref[...], ga_ref[...])
    xo_ref[...] = x_new
    if emit_h:
        h_ref = refs[pos]
        pos += 1
        h = _rms(x_new, gb_ref[...]) * (1.0 + sc_ref[...]) + sh_ref[...]
        h_ref[...] = h.astype(h_ref.dtype)
        if emit_logits:
            lg_ref = refs[pos]
            h_hi, h_lo = _split_bf16(h)
            w_hi, w_lo = _split_bf16(wr_ref[...])
            lg_ref[...] = (jnp.dot(h_hi, w_hi, preferred_element_type=F32)
                           + jnp.dot(h_lo, w_hi, preferred_element_type=F32)
                           + jnp.dot(h_hi, w_lo, preferred_element_type=F32))


def _resid_norm(x, y, gains, mod, la, ga_idx, gate_idx, nxt=None, h_dtype=BF16,
                w_router=None):
    B, S, D = x.shape
    ts = min(S, 256)
    row = pl.BlockSpec((None, ts, D), lambda b, i: (b, i, 0))
    in_specs = [row, row, _gain_spec(la, ga_idx, D), _mod_spec(la, gate_idx, D)]
    args = [x, y, gains, mod]
    out_specs = [row]
    out_shape = [jax.ShapeDtypeStruct((B, S, D), F32)]
    emit_h = nxt is not None
    emit_logits = w_router is not None
    if emit_h:
        lb, gb_idx, shift_idx, scale_idx = nxt
        in_specs += [_gain_spec(lb, gb_idx, D), _mod_spec(lb, scale_idx, D),
                     _mod_spec(lb, shift_idx, D)]
        args += [gains, mod, mod]
        out_specs.append(row)
        out_shape.append(jax.ShapeDtypeStruct((B, S, D), h_dtype))
    if emit_logits:
        in_specs.append(pl.BlockSpec((D, LANES), lambda b, i: (0, 0)))
        args.append(w_router)
        out_specs.append(pl.BlockSpec((None, ts, LANES), lambda b, i: (b, i, 0)))
        out_shape.append(jax.ShapeDtypeStruct((B, S, LANES), F32))
    outs = pl.pallas_call(
        functools.partial(_resid_norm_kernel, emit_h=emit_h, emit_logits=emit_logits),
        grid=(B, S // ts),
        in_specs=in_specs,
        out_specs=out_specs,
        out_shape=out_shape,
        compiler_params=_params(("arbitrary", "arbitrary"), 10 * ts * D * 4),
        name="resid_norm",
    )(*args)
    return outs


def _mm_kernel(a_ref, w_ref, o_ref):
    o_ref[...] = jnp.dot(a_ref[...], w_ref[...].astype(BF16),
                         preferred_element_type=F32).astype(o_ref.dtype)


def _nt_dot(a, wt):
    return lax.dot_general(a, wt, (((1,), (1,)), ((), ())), preferred_element_type=F32)


def _mm_nt_kernel(a_ref, wt_ref, o_ref):
    o_ref[...] = _nt_dot(a_ref[...], wt_ref[...].astype(BF16)).astype(o_ref.dtype)


def _qkv_kernel(a_ref, wt_ref, cs_ref, o_ref):
    acc = _nt_dot(a_ref[...], wt_ref[...].astype(BF16)) * cs_ref[...]
    for hh in range(o_ref.shape[0]):
        o_ref[hh] = acc[:, hh * HEAD_DIM:(hh + 1) * HEAD_DIM].astype(o_ref.dtype)


def _mm_two_kernel(a1_ref, a2_ref, w_ref, o_ref):
    k1 = a1_ref.shape[1]
    w = w_ref[...].astype(BF16)
    o_ref[...] = (jnp.dot(a1_ref[...], w[:k1], preferred_element_type=F32)
                  + jnp.dot(a2_ref[...], w[k1:], preferred_element_type=F32)).astype(o_ref.dtype)


def _gateup_kernel(a_ref, wg_ref, wu_ref, o_ref):
    a = a_ref[...]
    g = jnp.dot(a, wg_ref[...].astype(BF16), preferred_element_type=F32)
    u = jnp.dot(a, wu_ref[...].astype(BF16), preferred_element_type=F32)
    o_ref[...] = (g * jax.nn.sigmoid(g) * u).astype(o_ref.dtype)


def _w_spec(w, tn, lead):
    K = w.shape[-2]
    nlead = len(lead)
    return pl.BlockSpec((None,) * nlead + (K, tn), lambda i, j: tuple(lead) + (0, j))


def _a_spec(tm, K):
    return pl.BlockSpec((tm, K), lambda i, j: (i, 0), pipeline_mode=pl.Buffered(1))


def _dense_vmem(tm, K, tn, n_w, w_bytes, out_bytes, a_buffers):
    a = a_buffers * tm * K * 2
    w = n_w * (2 * K * tn * w_bytes + K * tn * 2)
    o = 2 * tm * tn * out_bytes + n_w * tm * tn * 4
    return a + w + o


def _qkv_proj(a, wt, l, n_cols, tm, tn, colscale):
    M, K = a.shape
    hpb = tn // HEAD_DIM
    return pl.pallas_call(
        _qkv_kernel,
        grid=(M // tm, n_cols // tn),
        in_specs=[_a_spec(tm, K),
                  pl.BlockSpec((None, tn, K), lambda i, j: (l, j, 0)),
                  pl.BlockSpec((1, tn), lambda i, j: (0, j))],
        out_specs=pl.BlockSpec((hpb, tm, HEAD_DIM), lambda i, j: (j, i, 0)),
        out_shape=jax.ShapeDtypeStruct((n_cols // HEAD_DIM, M, HEAD_DIM), BF16),
        compiler_params=_params(("arbitrary", "arbitrary"),
                                _dense_vmem(tm, K, tn, 1, wt.dtype.itemsize, 2, 1)),
        name="in_proj_qkv",
    )(a, wt, colscale)


def _matmul_nt(a, wt, tm, tn, out_dtype, name):
    M, K = a.shape
    n_cols = wt.shape[0]
    return pl.pallas_call(
        _mm_nt_kernel,
        grid=(M // tm, n_cols // tn),
        in_specs=[pl.BlockSpec((tm, K), lambda i, j: (i, 0)),
                  pl.BlockSpec((tn, K), lambda i, j: (j, 0))],
        out_specs=pl.BlockSpec((tm, tn), lambda i, j: (i, j)),
        out_shape=jax.ShapeDtypeStruct((M, n_cols), out_dtype),
        compiler_params=_params(
            ("arbitrary", "arbitrary"),
            _dense_vmem(tm, K, tn, 1, wt.dtype.itemsize, jnp.dtype(out_dtype).itemsize, 2)),
        name=name,
    )(a, wt)


def _matmul_two(a1, a2, w, lead, tm, tn, out_dtype, name):
    M, K1 = a1.shape
    K2 = a2.shape[1]
    n_cols = w.shape[-1]
    return pl.pallas_call(
        _mm_two_kernel,
        grid=(M // tm, n_cols // tn),
        in_specs=[_a_spec(tm, K1), _a_spec(tm, K2), _w_spec(w, tn, lead)],
        out_specs=pl.BlockSpec((tm, tn), lambda i, j: (i, j)),
        out_shape=jax.ShapeDtypeStruct((M, n_cols), out_dtype),
        compiler_params=_params(
            ("arbitrary", "arbitrary"),
            _dense_vmem(tm, K1 + K2, tn, 1, w.dtype.itemsize, jnp.dtype(out_dtype).itemsize, 1)),
        name=name,
    )(a1, a2, w)


def _gateup(a, wg, wu, lead, tm, tn, name="ffn_gate_up"):
    M, K = a.shape
    n_cols = wg.shape[-1]
    return pl.pallas_call(
        _gateup_kernel,
        grid=(M // tm, n_cols // tn),
        in_specs=[_a_spec(tm, K), _w_spec(wg, tn, lead), _w_spec(wu, tn, lead)],
        out_specs=pl.BlockSpec((tm, tn), lambda i, j: (i, j)),
        out_shape=jax.ShapeDtypeStruct((M, n_cols), BF16),
        compiler_params=_params(("arbitrary", "arbitrary"),
                                _dense_vmem(tm, K, tn, 2, wg.dtype.itemsize, 2, 1)),
        name=name,
    )(a, wg, wu)


def _down(a, w, lead, tm, tn, name="ffn_down"):
    M, K = a.shape
    n_cols = w.shape[-1]
    return pl.pallas_call(
        _mm_kernel,
        grid=(M // tm, n_cols // tn),
        in_specs=[_a_spec(tm, K), _w_spec(w, tn, lead)],
        out_specs=pl.BlockSpec((tm, tn), lambda i, j: (i, j)),
        out_shape=jax.ShapeDtypeStruct((M, n_cols), F32),
        compiler_params=_params(("arbitrary", "arbitrary"),
                                _dense_vmem(tm, K, tn, 1, w.dtype.itemsize, 4, 1)),
        name=name,
    )(a, w)


def _fox_prep_kernel(f_ref, b_ref, fsh_ref, fhs_ref):
    z = f_ref[...] + b_ref[...]
    acc = jnp.minimum(z, 0.0) - jnp.log1p(jnp.exp(-jnp.abs(z)))
    S = acc.shape[0]
    row = lax.broadcasted_iota(I32, acc.shape, 0)
    d = 1
    while d < S:
        acc = acc + jnp.where(row >= d, pltpu.roll(acc, d, 0), 0.0)
        d *= 2
    acc = acc * LOG2_E
    fsh_ref[...] = acc
    fhs_ref[...] = acc.T


def _fox_prep(fu3, b_pad, col_block):
    B, S, _ = fu3.shape
    return pl.pallas_call(
        _fox_prep_kernel,
        grid=(B,),
        in_specs=[pl.BlockSpec((None, S, LANES), lambda b: (b, 0, col_block)),
                  pl.BlockSpec((1, LANES), lambda b: (0, 0))],
        out_specs=[pl.BlockSpec((None, S, LANES), lambda b: (b, 0, 0)),
                   pl.BlockSpec((None, LANES, S), lambda b: (b, 0, 0))],
        out_shape=[jax.ShapeDtypeStruct((B, S, LANES), F32),
                   jax.ShapeDtypeStruct((B, LANES, S), F32)],
        compiler_params=_params(("arbitrary",), 12 * S * LANES * 4),
        name="fox_prep",
    )(fu3, b_pad)


def _attn_kernel(q_ref, k_ref, v_ref, fq_ref, fk_ref, o_ref, fq_s, *, blk):
    heads = q_ref.shape[0]
    for hh in range(heads):
        _attn_head(pl.program_id(1) * heads + hh, q_ref.at[hh], k_ref.at[hh], v_ref.at[hh],
                   fq_ref, fk_ref.at[hh], o_ref, hh * HEAD_DIM, fq_s.at[hh], blk)


def _attn_head(h, q_ref, k_ref, v_ref, fq_ref, fk_ref, o_ref, o_col, fq_s, blk):
    S = q_ref.shape[0]
    n_blk = S // blk
    lane = lax.broadcasted_iota(I32, fq_ref.shape, 1)
    fq_col = jnp.sum(jnp.where(lane == h, fq_ref[...], 0.0), axis=-1, keepdims=True)
    fq_s[...] = jnp.broadcast_to(fq_col, fq_s.shape)

    def logits(row_lo, n_rows, key_lo, n_keys):
        q = q_ref[row_lo:row_lo + n_rows, :]
        fq = fq_s[row_lo:row_lo + n_rows, :]
        s = lax.dot_general(q, k_ref[key_lo:key_lo + n_keys, :], (((1,), (1,)), ((), ())),
                            preferred_element_type=F32)
        chunks = []
        for c in range(n_keys // LANES):
            lo = key_lo + c * LANES
            sc = s[:, c * LANES:(c + 1) * LANES] + (fq - fk_ref[:, lo:lo + LANES])
            if lo + LANES - 1 > row_lo:
                r = lax.broadcasted_iota(I32, (n_rows, LANES), 0) + row_lo
                k = lax.broadcasted_iota(I32, (n_rows, LANES), 1) + lo
                sc = jnp.where(k <= r, sc, -jnp.inf)
            chunks.append(sc)
        return chunks

    def online_update(state, chunks, v):
        m_row = jnp.max(functools.reduce(jnp.maximum, chunks), axis=-1, keepdims=True)
        m_new = m_row if state is None else jnp.maximum(state[0], m_row)
        m_new = jnp.broadcast_to(m_new, chunks[0].shape)
        ps = [jnp.exp2(sc - m_new) for sc in chunks]
        p_sum = functools.reduce(lambda a, b: a + b, ps)
        pv = jnp.dot(jnp.concatenate(ps, axis=1).astype(BF16), v, preferred_element_type=F32)
        if state is None:
            return m_new, p_sum, pv
        alpha = jnp.exp2(state[0] - m_new)
        return m_new, alpha * state[1] + p_sum, alpha * state[2] + pv

    key_span = 2 * blk
    for i in range(n_blk):
        r0 = i * blk
        state = None
        n_prefix = r0 + blk
        for k0 in range(0, n_prefix, key_span):
            n_keys = min(key_span, n_prefix - k0)
            state = online_update(state, logits(r0, blk, k0, n_keys), v_ref[k0:k0 + n_keys, :])
        _, l_b, acc = state
        l_row = jnp.sum(l_b, axis=-1, keepdims=True)
        o_ref[r0:r0 + blk, o_col:o_col + HEAD_DIM] = (acc / l_row).astype(o_ref.dtype)


def _attention(qkv_h, f_sh, f_hs4, n_heads):
    _, B, S, _ = qkv_h.shape
    H = n_heads
    blk = min(S, 512)
    hps = 2 if H % 2 == 0 else 1
    head = lambda off: pl.BlockSpec((hps, None, S, HEAD_DIM),
                                    lambda b, g: (off // hps + g, b, 0, 0))
    return pl.pallas_call(
        functools.partial(_attn_kernel, blk=blk),
        grid=(B, H // hps),
        in_specs=[head(0), head(H), head(2 * H),
                  pl.BlockSpec((None, S, LANES), lambda b, g: (b, 0, 0)),
                  pl.BlockSpec((None, hps, 1, S), lambda b, g: (b, g, 0, 0))],
        out_specs=pl.BlockSpec((None, S, hps * HEAD_DIM), lambda b, g: (b, 0, g)),
        out_shape=jax.ShapeDtypeStruct((B, S, H * HEAD_DIM), BF16),
        scratch_shapes=[pltpu.VMEM((hps, S, LANES), F32)],
        compiler_params=_params(
            ("arbitrary", "arbitrary"),
            hps * (8 * S * HEAD_DIM * 2 + 3 * S * LANES * 4 + 8 * blk * blk * 4)),
        name="fox_attention",
    )(qkv_h, qkv_h, qkv_h, f_sh, f_hs4)


def _pool_kernel(u_ref, w_ref, sc_ref, o_ref):
    g = pl.program_id(1)
    u = u_ref[...]
    row = lax.broadcasted_iota(I32, u.shape, 0)

    def shifted(v, d):
        return jnp.where(row >= d, pltpu.roll(v, d, 0), 0.0)

    s2 = u + shifted(u, 1)
    s4 = s2 + shifted(s2, 2)
    s8 = s4 + shifted(s4, 4)
    s16 = s8 + shifted(s8, 8)
    win = jnp.where(g == 0, s2, jnp.where(g == 1, s4, jnp.where(g == 2, s8, s16)))
    width = lax.shift_left(jnp.int32(POOL_WINDOWS[0]), g)
    count = jnp.minimum(row + 1, width).astype(F32)
    pooled = win / count - u
    y = jnp.dot(pooled.astype(BF16), w_ref[...].astype(BF16), preferred_element_type=F32)
    o_ref[...] = (y * sc_ref[...]).astype(o_ref.dtype)


def _pool(fu3, w_pool, pool_scale3, l):
    B, S, _ = fu3.shape
    G, C = w_pool.shape[1], w_pool.shape[2]
    assert G == len(POOL_WINDOWS) and all(
        w == POOL_WINDOWS[0] << k for k, w in enumerate(POOL_WINDOWS))
    return pl.pallas_call(
        _pool_kernel,
        grid=(B, G),
        in_specs=[pl.BlockSpec((None, S, C), lambda b, g: (b, 0, g)),
                  pl.BlockSpec((None, None, C, C), lambda b, g: (l, g, 0, 0)),
                  pl.BlockSpec((None, 1, C), lambda b, g: (l, 0, g))],
        out_specs=pl.BlockSpec((None, S, C), lambda b, g: (b, 0, g)),
        out_shape=jax.ShapeDtypeStruct((B, S, G * C), BF16),
        compiler_params=_params(("arbitrary", "arbitrary"), 14 * S * C * 4),
        name="pool_mixer",
    )(fu3, w_pool, pool_scale3)


def _route_kernel(lg_ref, dest_ref, wts_ref, meta_ref, m_all, m_top, tri, *,
                  n_experts, tile, chunk):
    N = lg_ref.shape[0]
    n_chunks = N // chunk
    lane = lax.broadcasted_iota(I32, (chunk, LANES), 1).astype(F32)
    lane1 = lax.broadcasted_iota(I32, (1, LANES), 1).astype(F32)
    r = lax.broadcasted_iota(I32, (chunk, chunk), 0)
    c = lax.broadcasted_iota(I32, (chunk, chunk), 1)
    tri[...] = jnp.where(c < r, 1.0, 0.0).astype(BF16)

    def select(ci, counts):
        rows = pl.ds(pl.multiple_of(ci * chunk, chunk), chunk)
        lg = jnp.where(lane < n_experts, lg_ref[rows, :], -jnp.inf)
        v1 = jnp.max(lg, axis=-1, keepdims=True)
        i1 = jnp.min(jnp.where(lg == v1, lane, float(LANES)), axis=-1, keepdims=True)
        first = lane == i1
        lg2 = jnp.where(first, -jnp.inf, lg)
        v2 = jnp.max(lg2, axis=-1, keepdims=True)
        i2 = jnp.min(jnp.where(lg2 == v2, lane, float(LANES)), axis=-1, keepdims=True)
        second = lane == i2
        e2 = jnp.exp(v2 - v1)
        w1 = 1.0 / (1.0 + e2)
        w2 = e2 / (1.0 + e2)
        wts_ref[rows, :] = jnp.where(lane == 0.0, w1, jnp.where(lane == 1.0, w2, 0.0))
        both = jnp.where(first | second, 1.0, 0.0)
        m_all[rows, :] = both.astype(BF16)
        m_top[rows, :] = jnp.where(first, 1.0, 0.0).astype(BF16)
        return counts + jnp.sum(both, axis=0, keepdims=True)

    counts = lax.fori_loop(0, n_chunks, select, jnp.zeros((1, LANES), F32))
    padded = jnp.ceil(counts / tile) * tile
    start = jnp.zeros((1, LANES), F32)
    tile_row = lane1 * tile
    tile_expert = jnp.zeros((1, LANES), F32)
    last_used = jnp.zeros((1, 1), F32)
    running = jnp.zeros((1, 1), F32)
    tile_valid = jnp.zeros((1, LANES), F32)
    for e in range(n_experts):
        rows_e = jnp.sum(jnp.where(lane1 == e, padded, 0.0), axis=-1, keepdims=True)
        count_e = jnp.sum(jnp.where(lane1 == e, counts, 0.0), axis=-1, keepdims=True)
        in_group = (tile_row >= running) & (tile_row < running + rows_e)
        tile_valid = jnp.where(in_group, jnp.minimum(running + count_e - tile_row, float(tile)),
                               tile_valid)
        start = start + jnp.where(lane1 > e, rows_e, 0.0)
        running = running + rows_e
        tile_expert = tile_expert + jnp.where(tile_row >= running, 1.0, 0.0)
        last_used = jnp.where(rows_e > 0.0, float(e), last_used)
    tile_expert = jnp.minimum(tile_expert, last_used)
    n_tiles = running / tile
    sub = lax.broadcasted_iota(I32, (SUBLANES, LANES), 0)
    meta_ref[...] = jnp.where(sub == 0, tile_expert,
                              jnp.where(sub == 1, n_tiles,
                                        jnp.where(sub == 2, tile_valid, 0.0))).astype(I32)

    def place(ci, carry):
        rows = pl.ds(pl.multiple_of(ci * chunk, chunk), chunk)
        both = m_all[rows, :]
        top = m_top[rows, :].astype(F32)
        rank = jnp.dot(tri[...], both, preferred_element_type=F32)
        pos = start + carry + rank
        both_f = both.astype(F32)
        d1 = jnp.sum(top * pos, axis=-1, keepdims=True)
        d2 = jnp.sum((both_f - top) * pos, axis=-1, keepdims=True)
        dest_ref[rows, :] = jnp.where(lane == 0.0, d1, jnp.where(lane == 1.0, d2, 0.0)).astype(I32)
        return carry + jnp.sum(both_f, axis=0, keepdims=True)

    lax.fori_loop(0, n_chunks, place, jnp.zeros((1, LANES), F32))


def _route(logits, n_experts, tile):
    N = logits.shape[0]
    chunk = min(N, 512)
    return pl.pallas_call(
        functools.partial(_route_kernel, n_experts=n_experts, tile=tile, chunk=chunk),
        out_shape=[jax.ShapeDtypeStruct((N, LANES), I32),
                   jax.ShapeDtypeStruct((N, LANES), F32),
                   jax.ShapeDtypeStruct((SUBLANES, LANES), I32)],
        scratch_shapes=[pltpu.VMEM((N, LANES), BF16), pltpu.VMEM((N, LANES), BF16),
                        pltpu.VMEM((chunk, chunk), BF16)],
        compiler_params=pltpu.CompilerParams(
            vmem_limit_bytes=int(min(max(N * LANES * 4 * 8, 16 * 2**20), VMEM_CAP_BYTES))),
        name="moe_route",
    )(logits)


def _gather_kernel(dest_ref, nt_ref, h_hbm, o_ref, src, stage, sems, *, tile, n_rows, n_assign):
    t = pl.program_id(0)
    n_used = nt_ref[0]
    slot = t % 2

    def issue_tile(tt, to_slot):
        base = tt * tile

        def issue(rr, carry):
            tok = src[base + rr]
            pltpu.make_async_copy(h_hbm.at[pl.ds(tok, 1), :],
                                  stage.at[to_slot, pl.ds(rr, 1), :], sems.at[to_slot]).start()
            return carry
        lax.fori_loop(0, tile, issue, 0, unroll=8)

    @pl.when(t == 0)
    def _invert_dest():
        def clear(rr, carry):
            src[rr] = 0
            return carry
        lax.fori_loop(0, n_rows, clear, 0, unroll=8)

        def scatter(n, carry):
            for k in range(TOP_K):
                src[dest_ref[n * TOP_K + k]] = n
            return carry
        lax.fori_loop(0, n_assign // TOP_K, scatter, 0, unroll=4)
        issue_tile(0, 0)

    @pl.when(t + 1 < n_used)
    def _prefetch_next():
        issue_tile(t + 1, 1 - slot)

    @pl.when(t < n_used)
    def _gather():
        pltpu.make_async_copy(h_hbm.at[pl.ds(0, tile), :], stage.at[slot], sems.at[slot]).wait()
        o_ref[...] = stage[slot].astype(o_ref.dtype)

    @pl.when(t >= n_used)
    def _unused_tile():
        o_ref[...] = jnp.zeros(o_ref.shape, o_ref.dtype)


def _gather(dest_flat, n_tiles, h2, tile, max_tiles):
    N, D = h2.shape
    n_rows = max_tiles * tile
    grid_spec = pltpu.PrefetchScalarGridSpec(
        num_scalar_prefetch=2,
        grid=(max_tiles,),
        in_specs=[pl.BlockSpec(memory_space=pl.ANY)],
        out_specs=pl.BlockSpec((tile, D), lambda t, dest, nt: (t, 0)),
        scratch_shapes=[pltpu.SMEM((n_rows,), I32), pltpu.VMEM((2, tile, D), F32),
                        pltpu.SemaphoreType.DMA((2,))],
    )
    return pl.pallas_call(
        functools.partial(_gather_kernel, tile=tile, n_rows=n_rows, n_assign=dest_flat.shape[0]),
        grid_spec=grid_spec,
        out_shape=jax.ShapeDtypeStruct((n_rows, D), BF16),
        compiler_params=_params(("arbitrary",), 4 * tile * D * 4),
        name="moe_gather",
    )(dest_flat, n_tiles, h2)


def _grouped_kernel(te_ref, nt_ref, nv_ref, a_ref, *rest, n_w, layer, tn, max_tiles, compute):
    w_hbm = rest[:n_w]
    o_ref = rest[n_w]
    w_buf = rest[n_w + 1:2 * n_w + 1]
    sems, state = rest[2 * n_w + 1:]
    j = pl.program_id(0)
    t = pl.program_id(1)
    n_used = nt_ref[0]

    def fetch(e, jj, slot):
        return [pltpu.make_async_copy(w_hbm[wi].at[layer, e, :, pl.ds(jj * tn, tn)],
                                      w_buf[wi].at[slot], sems.at[wi, slot])
                for wi in range(n_w)]

    @pl.when((j == 0) & (t == 0))
    def _first_fetch():
        state[0] = 0
        for cp in fetch(te_ref[0], 0, 0):
            cp.start()

    e = te_ref[t]
    run_start = (t < n_used) & ((t == 0) | (e != te_ref[jnp.maximum(t - 1, 0)]))

    @pl.when(run_start)
    def _swap_weights():
        slot = state[0]
        for cp in fetch(e, j, slot):
            cp.wait()
        t_next = lax.while_loop(
            lambda tt: (tt < n_used) & (te_ref[jnp.minimum(tt, max_tiles - 1)] == e),
            lambda tt: tt + 1, t + 1)
        same_sweep = t_next < n_used

        @pl.when(same_sweep)
        def _next_expert():
            for cp in fetch(te_ref[jnp.minimum(t_next, max_tiles - 1)], j, 1 - slot):
                cp.start()

        @pl.when(jnp.logical_not(same_sweep) & (j + 1 < pl.num_programs(0)))
        def _next_sweep():
            for cp in fetch(te_ref[0], j + 1, 1 - slot):
                cp.start()

        state[1] = slot
        state[0] = 1 - slot

    @pl.when(t < n_used)
    def _compute():
        slot = state[1]
        weights = [wb.at[slot] for wb in w_buf]
        n_valid = nv_ref[t]
        tile = a_ref.shape[0]
        for rows in range(MOE_ROW_STEP, tile + 1, MOE_ROW_STEP):
            @pl.when((n_valid > rows - MOE_ROW_STEP) & (n_valid <= rows))
            def _leading_rows(rows=rows):
                compute(a_ref.at[pl.ds(0, rows)], *weights, o_ref.at[pl.ds(0, rows)])
                if rows < tile:
                    o_ref[pl.ds(rows, tile - rows), :] = jnp.zeros((tile - rows, tn), o_ref.dtype)

    @pl.when(t >= n_used)
    def _unused_tile():
        o_ref[...] = jnp.zeros(o_ref.shape, o_ref.dtype)


def _grouped(compute, a, ws, l, tile_expert, n_tiles, tile_valid, tile, tn, out_dtype, name):
    R, K = a.shape
    n_cols = ws[0].shape[-1]
    max_tiles = R // tile
    n_w = len(ws)
    assert tile % MOE_ROW_STEP == 0
    grid_spec = pltpu.PrefetchScalarGridSpec(
        num_scalar_prefetch=3,
        grid=(n_cols // tn, max_tiles),
        in_specs=[pl.BlockSpec((tile, K),
                               lambda j, t, te, nt, nv: (jnp.minimum(t, nt[0] - 1), 0))]
        + [pl.BlockSpec(memory_space=pl.ANY)] * n_w,
        out_specs=pl.BlockSpec((tile, tn), lambda j, t, te, nt, nv: (t, j)),
        scratch_shapes=[pltpu.VMEM((2, K, tn), ws[0].dtype) for _ in range(n_w)]
        + [pltpu.SemaphoreType.DMA((n_w, 2)), pltpu.SMEM((2,), I32)],
    )
    return pl.pallas_call(
        functools.partial(_grouped_kernel, n_w=n_w, layer=l, tn=tn, max_tiles=max_tiles,
                          compute=compute),
        grid_spec=grid_spec,
        out_shape=jax.ShapeDtypeStruct((R, n_cols), out_dtype),
        compiler_params=_params(
            ("arbitrary", "arbitrary"),
            _dense_vmem(tile, K, tn, n_w, ws[0].dtype.itemsize,
                        jnp.dtype(out_dtype).itemsize, 2)),
        name=name,
    )(tile_expert, n_tiles, tile_valid, a, *ws)


def _combine_kernel(dest_ref, ys_hbm, wts_ref, x_ref, g_ref, gate_ref, o_ref,
                    stage, sems, *, tile, tiles_per_seq, n_steps):
    t = pl.program_id(0) * tiles_per_seq + pl.program_id(1)
    slot = t % 2

    def issue_tile(tt, to_slot):
        n0 = tt * tile

        def issue(rr, carry):
            a = (n0 + rr) * TOP_K
            for k in range(TOP_K):
                pltpu.make_async_copy(ys_hbm.at[pl.ds(dest_ref[a + k], 1), :],
                                      stage.at[to_slot, k, pl.ds(rr, 1), :],
                                      sems.at[to_slot, k]).start()
            return carry
        lax.fori_loop(0, tile, issue, 0, unroll=4)

    @pl.when(t == 0)
    def _first_tile():
        issue_tile(0, 0)

    @pl.when(t + 1 < n_steps)
    def _prefetch_next():
        issue_tile(t + 1, 1 - slot)

    for k in range(TOP_K):
        pltpu.make_async_copy(ys_hbm.at[pl.ds(0, tile), :], stage.at[slot, k],
                              sems.at[slot, k]).wait()
    w = wts_ref[...]
    y = w[:, 0:1] * stage[slot, 0] + w[:, 1:2] * stage[slot, 1]
    o_ref[...] = x_ref[...] + gate_ref[...] * _rms(y, g_ref[...])


def _combine(dest_flat, ys, wts3, x, gains, mod, l, g_idx, gate_idx):
    B, S, D = x.shape
    tile = min(S, 256)
    row = pl.BlockSpec((None, tile, D), lambda b, i, dest: (b, i, 0))
    grid_spec = pltpu.PrefetchScalarGridSpec(
        num_scalar_prefetch=1,
        grid=(B, S // tile),
        in_specs=[pl.BlockSpec(memory_space=pl.ANY),
                  pl.BlockSpec((None, tile, LANES), lambda b, i, dest: (b, i, 0)),
                  row,
                  pl.BlockSpec((None, None, 1, D), lambda b, i, dest: (l, g_idx, 0, 0)),
                  pl.BlockSpec((None, None, None, 1, D),
                               lambda b, i, dest: (l, b, gate_idx, 0, 0))],
        out_specs=row,
        scratch_shapes=[pltpu.VMEM((2, TOP_K, tile, D), F32),
                        pltpu.SemaphoreType.DMA((2, TOP_K))],
    )
    return pl.pallas_call(
        functools.partial(_combine_kernel, tile=tile, tiles_per_seq=S // tile,
                          n_steps=B * (S // tile)),
        grid_spec=grid_spec,
        out_shape=jax.ShapeDtypeStruct((B, S, D), F32),
        compiler_params=_params(("arbitrary", "arbitrary"), 12 * tile * D * 4),
        name="moe_combine",
    )(dest_flat, ys, wts3, x, gains, mod)


def kernel(x, c, norm_gains, w_mod, b_mod, w_in, b_forget, w_pool, pool_scale, w_out,
           w_ffn_gate, w_ffn_up, w_ffn_down, w_router, w_exp_gate, w_exp_up, w_exp_down):
    B, S, D = x.shape
    L = w_mod.shape[0]
    N = B * S
    H = b_forget.shape[1]
    A = H * HEAD_DIM
    P = pool_scale.shape[1]
    assert A + P == D and w_in.shape[2] == 3 * A + H + P and H <= LANES
    assert P % LANES == 0 and A % LANES == 0
    E = w_router.shape[2]
    tm = min(N, 1024)
    tm_wide = min(N, 2048)
    tn_fu = 2 * LANES
    fu_cols = -(-(P + LANES) // tn_fu) * tn_fu

    mod = _mod(c, w_mod, b_mod)
    gains = norm_gains.reshape(L, 4, 1, D)
    h = _norm_mod(x, gains, mod, 0, 0, 0, 1, BF16)

    qk_scale = jnp.concatenate([jnp.full((1, A), HEAD_DIM ** -0.5 * LOG2_E, F32),
                                jnp.ones((1, 2 * A), F32)], axis=1)
    pool_scale3 = pool_scale.reshape(L, 1, P)
    w_in_t = jnp.swapaxes(w_in, 1, 2)

    for l in range(L):
        h2 = h.reshape(N, D)
        qkv_h = _qkv_proj(h2, w_in_t, l, 3 * A, tm_wide, min(3 * A, 512), qk_scale)
        w_fu_t = jnp.concatenate([w_in_t[l, 3 * A + H:], w_in_t[l, 3 * A:3 * A + H],
                                  jnp.zeros((fu_cols - P - H, D), F32)], axis=0)
        fu = _matmul_nt(h2, w_fu_t, tm_wide, tn_fu, F32, "in_proj_gate_pool")
        fu3 = fu.reshape(B, S, fu_cols)
        b_pad = jnp.pad(b_forget[l], (0, LANES - H)).reshape(1, LANES)
        f_sh, f_hs = _fox_prep(fu3, b_pad, P // LANES)
        attn = _attention(qkv_h.reshape(3 * H, B, S, HEAD_DIM), f_sh,
                          f_hs.reshape(B, LANES, 1, S), H)
        pooled = _pool(fu3, w_pool, pool_scale3, l)
        mixed = _matmul_two(attn.reshape(N, A), pooled.reshape(N, P), w_out, (l,), tm_wide,
                            min(D, 512), F32, "out_proj")

        i = l // 2
        if l % 2 == 0:
            x, h = _resid_norm(x, mixed.reshape(B, S, D), gains, mod, l, 1, 2,
                               nxt=(l, 2, 3, 4))
            FF = w_ffn_gate.shape[2]
            tn_ff = 256 if FF % 256 == 0 else LANES
            act = _gateup(h.reshape(N, D), w_ffn_gate, w_ffn_up, (i,), tm_wide, tn_ff)
            y = _down(act, w_ffn_down, (i,), tm, min(D, 256))
            y3 = y.reshape(B, S, D)
            if l + 1 < L:
                x, h = _resid_norm(x, y3, gains, mod, l, 3, 5, nxt=(l + 1, 0, 0, 1))
            else:
                (x,) = _resid_norm(x, y3, gains, mod, l, 3, 5)
        else:
            wr_pad = jnp.pad(w_router[i], ((0, 0), (0, LANES - E)))
            x, hf, logits = _resid_norm(x, mixed.reshape(B, S, D), gains, mod, l, 1, 2,
                                        nxt=(l, 2, 3, 4), h_dtype=F32, w_router=wr_pad)
            tile = min(N, 512)
            max_tiles = (N * TOP_K) // tile + E
            assert max_tiles <= LANES
            dest, wts, meta = _route(logits.reshape(N, LANES), E, tile)
            dest_flat = dest[:, :TOP_K].reshape(N * TOP_K)
            tile_expert = meta[0, :max_tiles]
            n_tiles = meta[1, :1]
            tile_valid = meta[2, :max_tiles]
            xs = _gather(dest_flat, n_tiles, hf.reshape(N, D), tile, max_tiles)
            tn_e = min(w_exp_gate.shape[3], 512)
            act = _grouped(_gateup_kernel, xs, (w_exp_gate, w_exp_up), i,
                           tile_expert, n_tiles, tile_valid, tile, tn_e, BF16, "moe_gate_up")
            ys = _grouped(_mm_kernel, act, (w_exp_down,), i,
                          tile_expert, n_tiles, tile_valid, tile, min(D, 1024), F32, "moe_down")
            x = _combine(dest_flat, ys, wts.reshape(B, S, LANES), x, gains, mod, l, 3, 5)
            if l + 1 < L:
                h = _norm_mod(x, gains, mod, l + 1, 0, 0, 1, BF16)
    return x
```
